```python
import jax, jax.numpy as jnp
from jax import lax
import numpy as np

D_MODEL = 4096
BATCH = 1
SEQ = 8192
DEPTH = 4

HEAD_DIM = 128
D_MIX = D_MODEL
D_ATTN = D_MIX // 2
N_Q_HEADS = D_ATTN // HEAD_DIM
N_KV_HEADS = N_Q_HEADS // 4
D_KV = N_KV_HEADS * HEAD_DIM
D_CONV = D_MIX - D_ATTN
CONV_GROUPS = D_CONV // HEAD_DIM
D_IN = D_ATTN + 2 * D_KV + 2 * D_CONV
CONV_WIDTH = 31
CONV_PAD = CONV_WIDTH // 2
WINDOW = 128
BLOCK = 128
N_SIDE = WINDOW // BLOCK
ROPE_DIM = HEAD_DIM // 4
ROPE_THETA = 500000.0
D_FF = 7 * D_MODEL // 4
NORM_EPS = 1e-6
FFN_RESID = 0.5

kernel_name = "hybrid_macaron_conv_window_gqa_encoder"


def rms_norm(x, g):
    xf = x.astype(jnp.float32)
    y = xf * lax.rsqrt(jnp.mean(xf * xf, axis=-1, keepdims=True) + NORM_EPS)
    return (y * g.astype(jnp.float32)).astype(x.dtype)


def swiglu(x, w_gate, w_up, w_down):
    return (jax.nn.silu(x @ w_gate) * (x @ w_up)) @ w_down


def partial_rope(x, positions):
    half = ROPE_DIM // 2
    inv_freq = ROPE_THETA ** (-jnp.arange(half, dtype=jnp.float32) * 2.0 / ROPE_DIM)
    ang = positions.astype(jnp.float32)[..., None] * inv_freq
    cos = jnp.cos(ang)[:, :, None, :]
    sin = jnp.sin(ang)[:, :, None, :]
    xr = x[..., :ROPE_DIM].astype(jnp.float32)
    x1, x2 = xr[..., :half], xr[..., half:]
    rot = jnp.concatenate([x1 * cos - x2 * sin, x2 * cos + x1 * sin], axis=-1)
    return jnp.concatenate([rot.astype(x.dtype), x[..., ROPE_DIM:]], axis=-1)


def banded_window_gqa(q, k, v, sink):
    b, s, hq, dh = q.shape
    hkv = k.shape[2]
    g = hq // hkv
    nb = s // BLOCK
    n_kb = 2 * N_SIDE + 1
    kw_len = n_kb * BLOCK
    qb = q.reshape(b, nb, BLOCK, hkv, g, dh)
    pad = ((0, 0), (N_SIDE * BLOCK, N_SIDE * BLOCK), (0, 0), (0, 0))
    kp = jnp.pad(k, pad).reshape(b, nb + 2 * N_SIDE, BLOCK, hkv, dh)
    vp = jnp.pad(v, pad).reshape(b, nb + 2 * N_SIDE, BLOCK, hkv, dh)
    kw = jnp.concatenate([kp[:, j:j + nb] for j in range(n_kb)], axis=2)
    vw = jnp.concatenate([vp[:, j:j + nb] for j in range(n_kb)], axis=2)
    scores = jnp.einsum('bnqhgd,bnkhd->bnhgqk', qb, kw,
                        preferred_element_type=jnp.float32) * (dh ** -0.5)
    blk = jnp.arange(nb, dtype=jnp.int32)[:, None]
    q_pos = blk * BLOCK + jnp.arange(BLOCK, dtype=jnp.int32)[None, :]
    k_pos = (blk - N_SIDE) * BLOCK + jnp.arange(kw_len, dtype=jnp.int32)[None, :]
    valid = ((jnp.abs(q_pos[:, :, None] - k_pos[:, None, :]) <= WINDOW)
             & (k_pos[:, None, :] >= 0) & (k_pos[:, None, :] < s))
    valid = valid[None, :, None, None, :, :]
    scores = jnp.where(valid, scores, -1e30)
    sink_f = sink.astype(jnp.float32).reshape(hkv, g)[None, None, :, :, None, None]
    m = jnp.maximum(jnp.max(scores, axis=-1, keepdims=True), sink_f)
    p = jnp.exp(scores - m)
    p = p / (jnp.sum(p, axis=-1, keepdims=True) + jnp.exp(sink_f - m))
    o = jnp.einsum('bnhgqk,bnkhd->bnqhgd', p.astype(v.dtype), vw)
    return o.reshape(b, s, hq * dh)


def conformer_conv(u, w_dw, b_dw, ln_g, ln_b, w_pw, b_pw):
    a, gate = jnp.split(u, 2, axis=-1)
    h = a * jax.nn.sigmoid(gate)
    h = lax.conv_general_dilated(
        h, w_dw[:, None, :].astype(h.dtype), window_strides=(1,),
        padding=[(CONV_PAD, CONV_PAD)], dimension_numbers=('NWC', 'WIO', 'NWC'),
        feature_group_count=h.shape[-1]) + b_dw
    hf = h.astype(jnp.float32)
    mu = jnp.mean(hf, axis=-1, keepdims=True)
    var = jnp.mean(jnp.square(hf - mu), axis=-1, keepdims=True)
    hn = (hf - mu) * lax.rsqrt(var + NORM_EPS) * ln_g.astype(jnp.float32) + ln_b.astype(jnp.float32)
    h = jax.nn.silu(hn).astype(u.dtype)
    return h @ w_pw + b_pw


def hybrid_mixer(xn, positions, w_in, attn_sink, conv_w_dw, conv_b_dw, conv_ln_g, conv_ln_b,
                 conv_w_pw, conv_b_pw, attn_out_gain, conv_out_gain, w_out):
    b, s, _ = xn.shape
    proj = xn @ w_in
    q, k, v, u = jnp.split(proj, [D_ATTN, D_ATTN + D_KV, D_ATTN + 2 * D_KV], axis=-1)
    q = partial_rope(q.reshape(b, s, N_Q_HEADS, HEAD_DIM), positions)
    k = partial_rope(k.reshape(b, s, N_KV_HEADS, HEAD_DIM), positions)
    v = v.reshape(b, s, N_KV_HEADS, HEAD_DIM)
    attn = banded_window_gqa(q, k, v, attn_sink)
    conv = conformer_conv(u, conv_w_dw, conv_b_dw, conv_ln_g, conv_ln_b, conv_w_pw, conv_b_pw)
    y = jnp.concatenate([rms_norm(attn, attn_out_gain), rms_norm(conv, conv_out_gain)], axis=-1)
    return y @ w_out


def setup_inputs(seed: int = 0) -> dict:
    key = jax.random.key(seed)
    ks = jax.random.split(key, 24)
    L, D, F = DEPTH, D_MODEL, D_FF
    f32 = jnp.float32

    def nrm(k, shape, scale):
        return jax.random.normal(k, shape, f32) * scale

    def gain(k, shape):
        return 1.0 + 0.01 * jax.random.normal(k, shape, f32)

    return {
        "x": jax.random.normal(ks[0], (BATCH, SEQ, D), f32),
        "positions": jnp.broadcast_to(jnp.arange(SEQ, dtype=jnp.int32), (BATCH, SEQ)),
        "ffn1_norm": gain(ks[1], (L, D)),
        "ffn1_w_gate": nrm(ks[2], (L, D, F), D ** -0.5),
        "ffn1_w_up": nrm(ks[3], (L, D, F), D ** -0.5),
        "ffn1_w_down": nrm(ks[4], (L, F, D), F ** -0.5),
        "mix_norm": gain(ks[5], (L, D)),
        "w_in": nrm(ks[6], (L, D, D_IN), D ** -0.5),
        "attn_sink": nrm(ks[7], (L, N_Q_HEADS), 1.0),
        "conv_w_dw": nrm(ks[8], (L, CONV_WIDTH, D_CONV), CONV_WIDTH ** -0.5),
        "conv_b_dw": nrm(ks[9], (L, D_CONV), 0.01),
        "conv_ln_g": gain(ks[10], (L, D_CONV)),
        "conv_ln_b": nrm(ks[11], (L, D_CONV), 0.01),
        "conv_w_pw": nrm(ks[12], (L, D_CONV, D_CONV), D_CONV ** -0.5),
        "conv_b_pw": nrm(ks[13], (L, D_CONV), 0.01),
        "attn_out_gain": gain(ks[14], (L, D_ATTN)),
        "conv_out_gain": gain(ks[15], (L, D_CONV)),
        "w_out": nrm(ks[16], (L, D_MIX, D), D_MIX ** -0.5),
        "ffn2_norm": gain(ks[17], (L, D)),
        "ffn2_w_gate": nrm(ks[18], (L, D, F), D ** -0.5),
        "ffn2_w_up": nrm(ks[19], (L, D, F), D ** -0.5),
        "ffn2_w_down": nrm(ks[20], (L, F, D), F ** -0.5),
        "final_norm": gain(ks[21], (D,)),
    }


def reference(x, positions, ffn1_norm, ffn1_w_gate, ffn1_w_up, ffn1_w_down, mix_norm, w_in,
              attn_sink, conv_w_dw, conv_b_dw, conv_ln_g, conv_ln_b, conv_w_pw, conv_b_pw,
              attn_out_gain, conv_out_gain, w_out, ffn2_norm, ffn2_w_gate, ffn2_w_up,
              ffn2_w_down, final_norm):
    h = x
    for l in range(DEPTH):
        h = h + FFN_RESID * swiglu(rms_norm(h, ffn1_norm[l]), ffn1_w_gate[l], ffn1_w_up[l], ffn1_w_down[l])
        h = h + hybrid_mixer(rms_norm(h, mix_norm[l]), positions, w_in[l], attn_sink[l],
                             conv_w_dw[l], conv_b_dw[l], conv_ln_g[l], conv_ln_b[l],
                             conv_w_pw[l], conv_b_pw[l], attn_out_gain[l], conv_out_gain[l], w_out[l])
        h = h + FFN_RESID * swiglu(rms_norm(h, ffn2_norm[l]), ffn2_w_gate[l], ffn2_w_up[l], ffn2_w_down[l])
    return rms_norm(h, final_norm)
```

```python
import functools

import jax
import jax.numpy as jnp
import numpy as np
from jax import lax
from jax.experimental import pallas as pl
from jax.experimental.pallas import tpu as pltpu

F32 = jnp.float32
BF16 = jnp.bfloat16

HEAD_DIM = 128
N_KV_HEADS = 4
GQA_GROUP = 4
D_KV = N_KV_HEADS * HEAD_DIM
CONV_WIDTH = 31
CONV_PAD = CONV_WIDTH // 2
ATTN_BLOCK = 128
ROPE_DIM = HEAD_DIM // 4
ROPE_THETA = 500000.0
NORM_EPS = 1e-6
FFN_RESID = 0.5
MASK_VALUE = -1e30

VMEM_LIMIT_BYTES = 58 * 1024 * 1024
HALO_ROWS = 16


def _params(n_axes):
    return pltpu.CompilerParams(
        dimension_semantics=("arbitrary",) * n_axes,
        vmem_limit_bytes=VMEM_LIMIT_BYTES)


def _sigmoid(x):
    return 1.0 / (1.0 + jnp.exp(-x))


def _silu(x):
    return x * _sigmoid(x)


def _bdot(a, b):
    return jnp.dot(a, b, preferred_element_type=F32)


def _rmsnorm_kernel(x_ref, g_ref, o_ref):
    x = x_ref[...]
    ms = jnp.mean(x * x, axis=-1, keepdims=True)
    o_ref[...] = (x * lax.rsqrt(ms + NORM_EPS) * g_ref[...]).astype(o_ref.dtype)


def _rmsnorm(x, gain3, layer, out_dtype, rows=256):
    m, d = x.shape
    return pl.pallas_call(
        _rmsnorm_kernel,
        grid=(m // rows,),
        in_specs=[pl.BlockSpec((rows, d), lambda i: (i, 0)),
                  pl.BlockSpec((None, 1, d), lambda i: (layer, 0, 0))],
        out_specs=pl.BlockSpec((rows, d), lambda i: (i, 0)),
        out_shape=jax.ShapeDtypeStruct((m, d), out_dtype),
        compiler_params=_params(1),
        name="rmsnorm",
    )(x, gain3)


def _gated_mm_kernel(act, x_ref, wa_ref, wb_ref, o_ref):
    x = x_ref[...]
    a = _bdot(x, wa_ref[...].astype(BF16))
    b = _bdot(x, wb_ref[...].astype(BF16))
    o_ref[...] = act(a, b).astype(o_ref.dtype)


def _gated_mm(x, wa, wb, layer, col_a, col_b, n_out, act, out_dtype, bm=1024, bn=256):
    m, k = x.shape
    return pl.pallas_call(
        functools.partial(_gated_mm_kernel, act),
        grid=(m // bm, n_out // bn),
        in_specs=[pl.BlockSpec((bm, k), lambda i, j: (i, 0)),
                  pl.BlockSpec((None, k, bn), lambda i, j: (layer, 0, j + col_a)),
                  pl.BlockSpec((None, k, bn), lambda i, j: (layer, 0, j + col_b))],
        out_specs=pl.BlockSpec((bm, bn), lambda i, j: (i, j)),
        out_shape=jax.ShapeDtypeStruct((m, n_out), out_dtype),
        compiler_params=_params(2),
        name="gated_mm",
    )(x, wa, wb)


def _swiglu_act(g, u):
    return _silu(g) * u


def _glu_act(a, gate):
    return a * _sigmoid(gate)


def _resid_mm_kernel(n_lhs, scale, *refs):
    lhs_refs = refs[:n_lhs]
    w_refs = refs[n_lhs:2 * n_lhs]
    r_ref, o_ref = refs[2 * n_lhs], refs[2 * n_lhs + 1]
    acc = _bdot(lhs_refs[0][...], w_refs[0][...].astype(BF16))
    for t in range(1, n_lhs):
        acc = acc + _bdot(lhs_refs[t][...], w_refs[t][...].astype(BF16))
    o_ref[...] = r_ref[...] + scale * acc


def _resid_mm(lhs_list, w, layer, resid, scale, bm=1024, bn=256):
    m, kk = lhs_list[0].shape
    n = resid.shape[1]
    n_lhs = len(lhs_list)
    in_specs = [pl.BlockSpec((bm, kk), lambda i, j: (i, 0)) for _ in lhs_list]
    in_specs += [pl.BlockSpec((None, kk, bn), functools.partial(
        lambda t, i, j: (layer, t, j), t)) for t in range(n_lhs)]
    in_specs += [pl.BlockSpec((bm, bn), lambda i, j: (i, j))]
    return pl.pallas_call(
        functools.partial(_resid_mm_kernel, n_lhs, scale),
        grid=(m // bm, n // bn),
        in_specs=in_specs,
        out_specs=pl.BlockSpec((bm, bn), lambda i, j: (i, j)),
        out_shape=jax.ShapeDtypeStruct((m, n), F32),
        compiler_params=_params(2),
        name="resid_mm",
    )(*lhs_list, *([w] * n_lhs), resid)


def _rope_table_kernel(pos_ref, freq_ref, c_ref, slo_ref, shi_ref):
    half = ROPE_DIM // 2
    ang = pos_ref[...].astype(F32) * freq_ref[...]
    cos, sin = jnp.cos(ang), jnp.sin(ang)
    lane = lax.broadcasted_iota(jnp.int32, ang.shape, 1)
    c_ref[...] = jnp.where(lane < ROPE_DIM, cos, 1.0)
    slo_ref[...] = jnp.where(lane < half, -sin, 0.0)
    shi_ref[...] = jnp.where((lane >= half) & (lane < ROPE_DIM), sin, 0.0)


def _rope_tables(positions, rows=512):
    m = positions.size
    half = ROPE_DIM // 2
    inv_freq = ROPE_THETA ** (-np.arange(half, dtype=np.float32) * np.float32(2.0) / ROPE_DIM)
    lane_freq = np.zeros((1, HEAD_DIM), np.float32)
    lane_freq[0, :half] = inv_freq
    lane_freq[0, half:ROPE_DIM] = inv_freq
    tab = jax.ShapeDtypeStruct((m, HEAD_DIM), F32)
    spec = pl.BlockSpec((rows, HEAD_DIM), lambda i: (i, 0))
    return pl.pallas_call(
        _rope_table_kernel,
        grid=(m // rows,),
        in_specs=[pl.BlockSpec((rows, 1), lambda i: (i, 0)),
                  pl.BlockSpec((1, HEAD_DIM), lambda i: (0, 0))],
        out_specs=[spec, spec, spec],
        out_shape=[tab, tab, tab],
        compiler_params=_params(1),
        name="rope_tables",
    )(positions.reshape(m, 1), jnp.asarray(lane_freq))


def _qkv_kernel(n_rope_tiles, x_ref, w_ref, c_ref, slo_ref, shi_ref, o_ref):
    j = pl.program_id(1)
    half = ROPE_DIM // 2
    y = _bdot(x_ref[...], w_ref[...].astype(BF16))

    @pl.when(j < n_rope_tiles)
    def _():
        c, slo, shi = c_ref[...], slo_ref[...], shi_ref[...]
        for h in range(y.shape[1] // HEAD_DIM):
            yh = y[:, h * HEAD_DIM:(h + 1) * HEAD_DIM]
            rot = (yh * c + pltpu.roll(yh, HEAD_DIM - half, 1) * slo
                   + pltpu.roll(yh, half, 1) * shi)
            o_ref[:, h * HEAD_DIM:(h + 1) * HEAD_DIM] = rot.astype(o_ref.dtype)

    @pl.when(j >= n_rope_tiles)
    def _():
        o_ref[...] = y.astype(o_ref.dtype)


def _qkv_proj(x, w_in, layer, tables, d_attn, bm=1024, bn=256):
    m, k = x.shape
    n_out = d_attn + 2 * D_KV
    n_rope_tiles = (d_attn + D_KV) // bn
    tspec = pl.BlockSpec((bm, HEAD_DIM), lambda i, j: (i, 0))
    return pl.pallas_call(
        functools.partial(_qkv_kernel, n_rope_tiles),
        grid=(m // bm, n_out // bn),
        in_specs=[pl.BlockSpec((bm, k), lambda i, j: (i, 0)),
                  pl.BlockSpec((None, k, bn), lambda i, j: (layer, 0, j)),
                  tspec, tspec, tspec],
        out_specs=pl.BlockSpec((bm, bn), lambda i, j: (i, j)),
        out_shape=jax.ShapeDtypeStruct((m, n_out), BF16),
        compiler_params=_params(2),
        name="qkv_proj",
    )(x, w_in, *tables)


def _attn_kernel(layer, n_blocks, sink_ref, q_ref, kp_ref, kc_ref, kn_ref,
                 vp_ref, vc_ref, vn_ref, g_ref, o_ref, acc_ref):
    n = pl.program_id(0)
    blk = ATTN_BLOCK
    rows = GQA_GROUP * blk
    r = lax.broadcasted_iota(jnp.int32, (rows, 3 * blk), 0) % blk
    c = lax.broadcasted_iota(jnp.int32, (rows, 3 * blk), 1)
    valid = (c >= r) & (c <= r + 2 * blk)
    valid = valid & ((c >= blk) | (n > 0)) & ((c < 2 * blk) | (n < n_blocks - 1))
    row_head = lax.broadcasted_iota(jnp.int32, (rows, 1), 0) // blk
    scale = HEAD_DIM ** -0.5
    ss = jnp.zeros((blk, 1), F32)
    for h in range(N_KV_HEADS):
        hs = slice(h * HEAD_DIM, (h + 1) * HEAD_DIM)
        qg = jnp.concatenate(
            [q_ref[:, (h * GQA_GROUP + g) * HEAD_DIM:(h * GQA_GROUP + g + 1) * HEAD_DIM]
             for g in range(GQA_GROUP)], axis=0)
        kw = jnp.concatenate([kp_ref[:, hs], kc_ref[:, hs], kn_ref[:, hs]], axis=0)
        vw = jnp.concatenate([vp_ref[:, hs], vc_ref[:, hs], vn_ref[:, hs]], axis=0)
        s = lax.dot_general(qg, kw, (((1,), (1,)), ((), ())),
                            preferred_element_type=F32) * scale
        s = jnp.where(valid, s, MASK_VALUE)
        sink = jnp.zeros((rows, 1), F32)
        for g in range(GQA_GROUP):
            sink = jnp.where(row_head == g, sink_ref[layer, h * GQA_GROUP + g], sink)
        mx = jnp.maximum(jnp.max(s, axis=-1, keepdims=True), sink)
        p = jnp.exp(s - mx)
        den = jnp.sum(p, axis=-1, keepdims=True) + jnp.exp(sink - mx)
        o = _bdot(p.astype(BF16), vw) / den
        for g in range(GQA_GROUP):
            og = o[g * blk:(g + 1) * blk]
            ss = ss + jnp.sum(og * og, axis=-1, keepdims=True)
            col = (h * GQA_GROUP + g) * HEAD_DIM
            acc_ref[:, col:col + HEAD_DIM] = og
    inv = lax.rsqrt(ss / acc_ref.shape[1] + NORM_EPS)
    o_ref[...] = (acc_ref[...] * inv * g_ref[...]).astype(o_ref.dtype)


def _attention(qkv, sink, gain3, layer, d_attn):
    m = qkv.shape[0]
    blk = ATTN_BLOCK
    nb = m // blk
    kcol = d_attn // D_KV
    vcol = kcol + 1
    prev = lambda n: jnp.maximum(n - 1, 0)
    nxt = lambda n: jnp.minimum(n + 1, nb - 1)
    kv = lambda rowf, col: pl.BlockSpec((blk, D_KV), lambda n: (rowf(n), col))
    same = lambda n: n
    return pl.pallas_call(
        functools.partial(_attn_kernel, layer, nb),
        grid=(nb,),
        in_specs=[pl.BlockSpec(memory_space=pltpu.SMEM),
                  pl.BlockSpec((blk, d_attn), lambda n: (n, 0)),
                  kv(prev, kcol), kv(same, kcol), kv(nxt, kcol),
                  kv(prev, vcol), kv(same, vcol), kv(nxt, vcol),
                  pl.BlockSpec((None, 1, d_attn), lambda n: (layer, 0, 0))],
        out_specs=pl.BlockSpec((blk, d_attn), lambda n: (n, 0)),
        out_shape=jax.ShapeDtypeStruct((m, d_attn), BF16),
        scratch_shapes=[pltpu.VMEM((blk, d_attn), F32)],
        compiler_params=_params(1),
        name="window_attn",
    )(sink, qkv, qkv, qkv, qkv, qkv, qkv, qkv, gain3)


def _conv_kernel(n_tiles, lane_chunk, prev_ref, main_ref, next_ref, w_ref, b_ref,
                 g_ref, beta_ref, o_ref, hbuf, cbuf):
    i = pl.program_id(0)
    tm, ch = main_ref.shape
    hbuf[0:HALO_ROWS, :] = jnp.where(i > 0, prev_ref[...], 0.0)
    hbuf[HALO_ROWS:HALO_ROWS + tm, :] = main_ref[...]
    hbuf[HALO_ROWS + tm:2 * HALO_ROWS + tm, :] = jnp.where(i < n_tiles - 1, next_ref[...], 0.0)
    base = HALO_ROWS - CONV_PAD

    def chunk(cidx, carry):
        c0 = pl.multiple_of(cidx * lane_chunk, lane_chunk)
        cs = pl.ds(c0, lane_chunk)
        acc = w_ref[0:1, cs] * hbuf[base:base + tm, cs]
        for k in range(1, CONV_WIDTH):
            acc = acc + w_ref[k:k + 1, cs] * hbuf[base + k:base + k + tm, cs]
        cbuf[:, cs] = acc + b_ref[:, cs]
        return carry

    lax.fori_loop(0, ch // lane_chunk, chunk, 0)
    h = cbuf[...]
    mu = jnp.mean(h, axis=-1, keepdims=True)
    d = h - mu
    var = jnp.mean(d * d, axis=-1, keepdims=True)
    hn = d * lax.rsqrt(var + NORM_EPS) * g_ref[...] + beta_ref[...]
    o_ref[...] = _silu(hn).astype(o_ref.dtype)


def _conv_front(h, w_dw, b3, g3, beta3, layer, tm=256, lane_chunk=128):
    m, ch = h.shape
    n_tiles = m // tm
    per = tm // HALO_ROWS
    n_halo = m // HALO_ROWS
    vec = pl.BlockSpec((None, 1, ch), lambda i: (layer, 0, 0))
    return pl.pallas_call(
        functools.partial(_conv_kernel, n_tiles, lane_chunk),
        grid=(n_tiles,),
        in_specs=[pl.BlockSpec((HALO_ROWS, ch), lambda i: (jnp.maximum(i * per - 1, 0), 0)),
                  pl.BlockSpec((tm, ch), lambda i: (i, 0)),
                  pl.BlockSpec((HALO_ROWS, ch),
                               lambda i: (jnp.minimum((i + 1) * per, n_halo - 1), 0)),
                  pl.BlockSpec((None, CONV_WIDTH, ch), lambda i: (layer, 0, 0)),
                  vec, vec, vec],
        out_specs=pl.BlockSpec((tm, ch), lambda i: (i, 0)),
        out_shape=jax.ShapeDtypeStruct((m, ch), BF16),
        scratch_shapes=[pltpu.VMEM((tm + 2 * HALO_ROWS, ch), F32),
                        pltpu.VMEM((tm, ch), F32)],
        compiler_params=_params(1),
        name="conv_front",
    )(h, h, h, w_dw, b3, g3, beta3)


def _pw_kernel(x_ref, w_ref, b_ref, g_ref, o_ref):
    y = _bdot(x_ref[...], w_ref[...].astype(BF16)) + b_ref[...]
    ms = jnp.mean(y * y, axis=-1, keepdims=True)
    o_ref[...] = (y * lax.rsqrt(ms + NORM_EPS) * g_ref[...]).astype(o_ref.dtype)


def _pointwise(x, w_pw, b3, g3, layer, bm=512):
    m, k = x.shape
    n = w_pw.shape[2]
    vec = pl.BlockSpec((None, 1, n), lambda i: (layer, 0, 0))
    return pl.pallas_call(
        _pw_kernel,
        grid=(m // bm,),
        in_specs=[pl.BlockSpec((bm, k), lambda i: (i, 0)),
                  pl.BlockSpec((None, k, n), lambda i: (layer, 0, 0)),
                  vec, vec],
        out_specs=pl.BlockSpec((bm, n), lambda i: (i, 0)),
        out_shape=jax.ShapeDtypeStruct((m, n), BF16),
        compiler_params=_params(1),
        name="pointwise",
    )(x, w_pw, b3, g3)


def kernel(x, positions, ffn1_norm, ffn1_w_gate, ffn1_w_up, ffn1_w_down, mix_norm, w_in,
           attn_sink, conv_w_dw, conv_b_dw, conv_ln_g, conv_ln_b, conv_w_pw, conv_b_pw,
           attn_out_gain, conv_out_gain, w_out, ffn2_norm, ffn2_w_gate, ffn2_w_up,
           ffn2_w_down, final_norm):
    b, s, d = x.shape
    depth = w_in.shape[0]
    d_ff = ffn1_w_gate.shape[2]
    d_attn = attn_out_gain.shape[1]
    d_conv = conv_out_gain.shape[1]
    assert b == 1 and positions.shape == (b, s)
    assert w_in.shape[2] == d_attn + 2 * D_KV + 2 * d_conv

    as3 = lambda v: v.reshape(v.shape[0], 1, v.shape[1])
    bn = 256
    tables = _rope_tables(positions)
    h = x.reshape(s, d)

    def ffn(h, norm3, wg, wu, wd, l):
        xn = _rmsnorm(h, norm3, l, BF16)
        act = _gated_mm(xn, wg, wu, l, 0, 0, d_ff, _swiglu_act, BF16)
        return _resid_mm([act], wd, l, h, FFN_RESID)

    ffn1_norm3, ffn2_norm3, mix_norm3 = as3(ffn1_norm), as3(ffn2_norm), as3(mix_norm)
    b_dw3, ln_g3, ln_b3, b_pw3 = as3(conv_b_dw), as3(conv_ln_g), as3(conv_ln_b), as3(conv_b_pw)
    attn_gain3, conv_gain3 = as3(attn_out_gain), as3(conv_out_gain)

    for l in range(depth):
        h = ffn(h, ffn1_norm3, ffn1_w_gate, ffn1_w_up, ffn1_w_down, l)
        xn = _rmsnorm(h, mix_norm3, l, BF16)
        qkv = _qkv_proj(xn, w_in, l, tables, d_attn)
        u_off = (d_attn + 2 * D_KV) // bn
        glu = _gated_mm(xn, w_in, w_in, l, u_off, u_off + d_conv // bn, d_conv,
                        _glu_act, F32)
        attn = _attention(qkv, attn_sink, attn_gain3, l, d_attn)
        conv = _conv_front(glu, conv_w_dw, b_dw3, ln_g3, ln_b3, l)
        conv = _pointwise(conv, conv_w_pw, b_pw3, conv_gain3, l)
        h = _resid_mm([attn, conv], w_out, l, h, 1.0)
        h = ffn(h, ffn2_norm3, ffn2_w_gate, ffn2_w_up, ffn2_w_down, l)

    out = _rmsnorm(h, final_norm.reshape(1, 1, d), 0, F32)
    return out.reshape(b, s, d)
```

```python
import functools
import math

import jax
import jax.numpy as jnp
import numpy as np
from jax import lax
from jax.experimental import pallas as pl
from jax.experimental.pallas import tpu as pltpu

F32 = jnp.float32
BF16 = jnp.bfloat16

LANES = 128
HEAD_DIM = 128
N_KV_HEADS = 4
GQA_GROUP = 4
D_KV = N_KV_HEADS * HEAD_DIM
CONV_WIDTH = 31
CONV_PAD = CONV_WIDTH // 2
ATTN_BLOCK = 128
ROPE_DIM = HEAD_DIM // 4
ROPE_THETA = 500000.0
NORM_EPS = 1e-6
FFN_RESID = 0.5
MASK_VALUE = -1e30
LOG2E = math.log2(math.e)

VMEM_LIMIT_BYTES = 58 * 1024 * 1024
HALO_ROWS = 16


def _params(n_axes):
    return pltpu.CompilerParams(
        dimension_semantics=("arbitrary",) * n_axes,
        vmem_limit_bytes=VMEM_LIMIT_BYTES)


def _sigmoid(x):
    return 1.0 / (1.0 + jnp.exp(-x))


def _silu(x):
    return x * _sigmoid(x)


def _bdot(a, b):
    return jnp.dot(a, b, preferred_element_type=F32)


def _lane_fold(x):
    acc = x[:, :LANES]
    for c in range(1, x.shape[1] // LANES):
        acc = acc + x[:, c * LANES:(c + 1) * LANES]
    return acc


def _row_rstd(ssq_ref, d):
    return lax.rsqrt(jnp.sum(ssq_ref[...], axis=-1, keepdims=True) / d + NORM_EPS)


def _norm_prep_kernel(x_ref, g_ref, hg_ref, ssq_ref):
    x = x_ref[...]
    hg_ref[...] = (x * g_ref[...]).astype(hg_ref.dtype)
    ssq_ref[...] = _lane_fold(x * x)


def _norm_prep(x, gain3, layer, rows=256):
    m, d = x.shape
    return pl.pallas_call(
        _norm_prep_kernel,
        grid=(m // rows,),
        in_specs=[pl.BlockSpec((rows, d), lambda i: (i, 0)),
                  pl.BlockSpec((None, 1, d), lambda i: (layer, 0, 0))],
        out_specs=[pl.BlockSpec((rows, d), lambda i: (i, 0)),
                   pl.BlockSpec((rows, LANES), lambda i: (i, 0))],
        out_shape=[jax.ShapeDtypeStruct((m, d), BF16),
                   jax.ShapeDtypeStruct((m, LANES), F32)],
        compiler_params=_params(1),
        name="norm_prep",
    )(x, gain3)


def _rmsnorm_kernel(x_ref, g_ref, o_ref):
    x = x_ref[...]
    ms = jnp.mean(x * x, axis=-1, keepdims=True)
    o_ref[...] = (x * lax.rsqrt(ms + NORM_EPS) * g_ref[...]).astype(o_ref.dtype)


def _rmsnorm(x, gain3, layer, out_dtype, rows=256):
    m, d = x.shape
    return pl.pallas_call(
        _rmsnorm_kernel,
        grid=(m // rows,),
        in_specs=[pl.BlockSpec((rows, d), lambda i: (i, 0)),
                  pl.BlockSpec((None, 1, d), lambda i: (layer, 0, 0))],
        out_specs=pl.BlockSpec((rows, d), lambda i: (i, 0)),
        out_shape=jax.ShapeDtypeStruct((m, d), out_dtype),
        compiler_params=_params(1),
        name="rmsnorm",
    )(x, gain3)


def _gated_mm_kernel(act, chunk_major, x_ref, ssq_ref, wa_ref, wb_ref, o_ref):
    x = x_ref[...]
    rstd = _row_rstd(ssq_ref, x.shape[1])
    a = _bdot(x, wa_ref[...].astype(BF16)) * rstd
    b = _bdot(x, wb_ref[...].astype(BF16)) * rstd
    y = act(a, b).astype(o_ref.dtype)
    if chunk_major:
        for c in range(y.shape[1] // LANES):
            o_ref[c] = y[:, c * LANES:(c + 1) * LANES]
    else:
        o_ref[...] = y


def _gated_mm(hg, ssq, wa, wb, layer, col_a, col_b, n_out, act, out_dtype, bn,
              chunk_major=False, bm=1024, lhs_buffers=2):
    m, k = hg.shape
    lhs_mode = pl.Buffered(lhs_buffers)
    if chunk_major:
        out_spec = pl.BlockSpec((bn // LANES, bm, LANES), lambda i, j: (j, i, 0))
        out_shape = jax.ShapeDtypeStruct((n_out // LANES, m, LANES), out_dtype)
    else:
        out_spec = pl.BlockSpec((bm, bn), lambda i, j: (i, j))
        out_shape = jax.ShapeDtypeStruct((m, n_out), out_dtype)
    return pl.pallas_call(
        functools.partial(_gated_mm_kernel, act, chunk_major),
        grid=(m // bm, n_out // bn),
        in_specs=[pl.BlockSpec((bm, k), lambda i, j: (i, 0), pipeline_mode=lhs_mode),
                  pl.BlockSpec((bm, LANES), lambda i, j: (i, 0)),
                  pl.BlockSpec((None, k, bn), lambda i, j: (layer, 0, j + col_a)),
                  pl.BlockSpec((None, k, bn), lambda i, j: (layer, 0, j + col_b))],
        out_specs=out_spec,
        out_shape=out_shape,
        compiler_params=_params(2),
        name="gated_mm",
    )(hg, ssq, wa, wb)


def _swiglu_act(g, u):
    return _silu(g) * u


def _glu_act(a, gate):
    return a * _sigmoid(gate)


def _resid_mm_kernel(n_lhs, scale, emit_norm, *refs):
    lhs_refs = refs[:n_lhs]
    w_refs = refs[n_lhs:2 * n_lhs]
    r_ref = refs[2 * n_lhs]
    acc = _bdot(lhs_refs[0][...], w_refs[0][...].astype(BF16))
    for t in range(1, n_lhs):
        acc = acc + _bdot(lhs_refs[t][...], w_refs[t][...].astype(BF16))
    if scale != 1.0:
        acc = scale * acc
    h = r_ref[...] + acc
    if not emit_norm:
        refs[2 * n_lhs + 1][...] = h
        return
    g_ref, o_ref, hg_ref, ssq_ref = refs[2 * n_lhs + 1:]
    o_ref[...] = h
    hg_ref[...] = (h * g_ref[...]).astype(hg_ref.dtype)
    part = _lane_fold(h * h)
    j = pl.program_id(1)

    @pl.when(j == 0)
    def _():
        ssq_ref[...] = part

    @pl.when(j > 0)
    def _():
        ssq_ref[...] += part


def _resid_mm(lhs_list, w, layer, resid, scale, next_gain, bn, bm=1024):
    m, kk = lhs_list[0].shape
    n = resid.shape[1]
    n_lhs = len(lhs_list)
    tile = pl.BlockSpec((bm, bn), lambda i, j: (i, j))
    in_specs = [pl.BlockSpec((bm, kk), lambda i, j: (i, 0)) for _ in lhs_list]
    in_specs += [pl.BlockSpec((None, kk, bn), functools.partial(
        lambda t, i, j: (layer, t, j), t)) for t in range(n_lhs)]
    in_specs += [tile]
    args = [*lhs_list, *([w] * n_lhs), resid]
    h_shape = jax.ShapeDtypeStruct((m, n), F32)
    if next_gain is None:
        out_specs, out_shape = tile, h_shape
    else:
        gain3, gl = next_gain
        in_specs += [pl.BlockSpec((None, 1, bn), lambda i, j: (gl, 0, j))]
        args += [gain3]
        out_specs = [tile, tile, pl.BlockSpec((bm, LANES), lambda i, j: (i, 0))]
        out_shape = [h_shape, jax.ShapeDtypeStruct((m, n), BF16),
                     jax.ShapeDtypeStruct((m, LANES), F32)]
    return pl.pallas_call(
        functools.partial(_resid_mm_kernel, n_lhs, scale, next_gain is not None),
        grid=(m // bm, n // bn),
        in_specs=in_specs,
        out_specs=out_specs,
        out_shape=out_shape,
        compiler_params=_params(2),
        name="resid_mm",
    )(*args)


def _rope_table_kernel(pos_ref, freq_ref, c_ref, slo_ref, shi_ref):
    half = ROPE_DIM // 2
    ang = pos_ref[...].astype(F32) * freq_ref[...]
    cos, sin = jnp.cos(ang), jnp.sin(ang)
    lane = lax.broadcasted_iota(jnp.int32, ang.shape, 1)
    c_ref[...] = jnp.where(lane < ROPE_DIM, cos, 1.0)
    slo_ref[...] = jnp.where(lane < half, -sin, 0.0)
    shi_ref[...] = jnp.where((lane >= half) & (lane < ROPE_DIM), sin, 0.0)


def _rope_tables(positions, rows=512):
    m = positions.size
    half = ROPE_DIM // 2
    inv_freq = ROPE_THETA ** (-np.arange(half, dtype=np.float32) * np.float32(2.0) / ROPE_DIM)
    lane_freq = np.zeros((1, HEAD_DIM), np.float32)
    lane_freq[0, :half] = inv_freq
    lane_freq[0, half:ROPE_DIM] = inv_freq
    tab = jax.ShapeDtypeStruct((m, HEAD_DIM), F32)
    spec = pl.BlockSpec((rows, HEAD_DIM), lambda i: (i, 0))
    return pl.pallas_call(
        _rope_table_kernel,
        grid=(m // rows,),
        in_specs=[pl.BlockSpec((rows, 1), lambda i: (i, 0)),
                  pl.BlockSpec((1, HEAD_DIM), lambda i: (0, 0))],
        out_specs=[spec, spec, spec],
        out_shape=[tab, tab, tab],
        compiler_params=_params(1),
        name="rope_tables",
    )(positions.reshape(m, 1), jnp.asarray(lane_freq))


def _qkv_kernel(n_rope_tiles, n_slabs, x_ref, ssq_ref, w_ref, c_ref, slo_ref, shi_ref, o_ref):
    half = ROPE_DIM // 2
    is_rope = (pl.program_id(1) < n_rope_tiles).astype(F32)
    w = w_ref[...].astype(BF16)
    bm, bn = o_ref.shape
    sm = bm // n_slabs
    for s in range(n_slabs):
        rows = slice(s * sm, (s + 1) * sm)
        rstd = _row_rstd(ssq_ref.at[rows, :], x_ref.shape[1])
        y = _bdot(x_ref[rows, :], w) * rstd
        c = 1.0 + is_rope * (c_ref[rows, :] - 1.0)
        slo = is_rope * slo_ref[rows, :]
        shi = is_rope * shi_ref[rows, :]
        for h in range(bn // HEAD_DIM):
            yh = y[:, h * HEAD_DIM:(h + 1) * HEAD_DIM]
            rot = (yh * c + pltpu.roll(yh, HEAD_DIM - half, 1) * slo
                   + pltpu.roll(yh, half, 1) * shi)
            o_ref[rows, h * HEAD_DIM:(h + 1) * HEAD_DIM] = rot.astype(o_ref.dtype)


def _qkv_proj(hg, ssq, w_in, layer, tables, d_attn, bn, bm=1024, n_slabs=2):
    m, k = hg.shape
    n_out = d_attn + 2 * D_KV
    n_rope_tiles = (d_attn + D_KV) // bn
    tspec = pl.BlockSpec((bm, HEAD_DIM), lambda i, j: (i, 0))
    return pl.pallas_call(
        functools.partial(_qkv_kernel, n_rope_tiles, n_slabs),
        grid=(m // bm, n_out // bn),
        in_specs=[pl.BlockSpec((bm, k), lambda i, j: (i, 0)),
                  pl.BlockSpec((bm, LANES), lambda i, j: (i, 0)),
                  pl.BlockSpec((None, k, bn), lambda i, j: (layer, 0, j)),
                  tspec, tspec, tspec],
        out_specs=pl.BlockSpec((bm, bn), lambda i, j: (i, j)),
        out_shape=jax.ShapeDtypeStruct((m, n_out), BF16),
        compiler_params=_params(2),
        name="qkv_proj",
    )(hg, ssq, w_in, *tables)


def _attn_kernel(layer, n_blocks, sink_ref, q_ref, kp_ref, kc_ref, kn_ref,
                 vp_ref, vc_ref, vn_ref, g_ref, o_ref, acc_ref):
    n = pl.program_id(0)
    blk = ATTN_BLOCK
    r = lax.broadcasted_iota(jnp.int32, (blk, blk), 0)
    c = lax.broadcasted_iota(jnp.int32, (blk, blk), 1)
    prev_ok = (c >= r) & (n > 0)
    next_ok = (c <= r) & (n < n_blocks - 1)
    scale2 = HEAD_DIM ** -0.5 * LOG2E
    ones = jnp.ones((3 * blk, HEAD_DIM), BF16)
    ss = jnp.zeros((blk, 1), F32)
    for h in range(N_KV_HEADS):
        hs = slice(h * HEAD_DIM, (h + 1) * HEAD_DIM)
        kw = jnp.concatenate([kp_ref[:, hs], kc_ref[:, hs], kn_ref[:, hs]], axis=0)
        vw = jnp.concatenate([vp_ref[:, hs], vc_ref[:, hs], vn_ref[:, hs]], axis=0)
        vw1 = jnp.concatenate([vw, ones], axis=1)
        for g in range(GQA_GROUP):
            qs = slice((h * GQA_GROUP + g) * HEAD_DIM, (h * GQA_GROUP + g + 1) * HEAD_DIM)
            s = lax.dot_general(q_ref[:, qs], kw, (((1,), (1,)), ((), ())),
                                preferred_element_type=F32) * scale2
            s_prev = jnp.where(prev_ok, s[:, :blk], MASK_VALUE)
            s_cur = s[:, blk:2 * blk]
            s_next = jnp.where(next_ok, s[:, 2 * blk:], MASK_VALUE)
            sink2 = sink_ref[layer, h * GQA_GROUP + g] * LOG2E
            mx = jnp.max(jnp.maximum(jnp.maximum(s_prev, s_cur), s_next),
                         axis=-1, keepdims=True)
            mx = jnp.maximum(mx, sink2)
            p = jnp.concatenate([jnp.exp2(s_prev - mx), jnp.exp2(s_cur - mx),
                                 jnp.exp2(s_next - mx)], axis=1).astype(BF16)
            pv = _bdot(p, vw1)
            den = pv[:, HEAD_DIM:] + jnp.exp2(sink2 - mx)
            o = pv[:, :HEAD_DIM] / den
            ss = ss + jnp.sum(o * o, axis=-1, keepdims=True)
            acc_ref[:, qs] = o
    inv = lax.rsqrt(ss / acc_ref.shape[1] + NORM_EPS)
    o_ref[...] = (acc_ref[...] * inv * g_ref[...]).astype(o_ref.dtype)


def _attention(qkv, sink, gain3, layer, d_attn):
    m = qkv.shape[0]
    blk = ATTN_BLOCK
    nb = m // blk
    kcol = d_attn // D_KV
    vcol = kcol + 1
    prev = lambda n: jnp.maximum(n - 1, 0)
    nxt = lambda n: jnp.minimum(n + 1, nb - 1)
    kv = lambda rowf, col: pl.BlockSpec((blk, D_KV), lambda n: (rowf(n), col))
    same = lambda n: n
    return pl.pallas_call(
        functools.partial(_attn_kernel, layer, nb),
        grid=(nb,),
        in_specs=[pl.BlockSpec(memory_space=pltpu.SMEM),
                  pl.BlockSpec((blk, d_attn), lambda n: (n, 0)),
                  kv(prev, kcol), kv(same, kcol), kv(nxt, kcol),
                  kv(prev, vcol), kv(same, vcol), kv(nxt, vcol),
                  pl.BlockSpec((None, 1, d_attn), lambda n: (layer, 0, 0))],
        out_specs=pl.BlockSpec((blk, d_attn), lambda n: (n, 0)),
        out_shape=jax.ShapeDtypeStruct((m, d_attn), BF16),
        scratch_shapes=[pltpu.VMEM((blk, d_attn), F32)],
        compiler_params=_params(1),
        name="window_attn",
    )(sink, qkv, qkv, qkv, qkv, qkv, qkv, qkv, gain3)


def _conv_kernel(n_tiles, prev_ref, main_ref, next_ref, w_ref, b_ref,
                 g_ref, beta_ref, o_ref, hbuf, cbuf):
    i = pl.program_id(0)
    nch, tm, _ = main_ref.shape
    hbuf[:, 0:HALO_ROWS, :] = jnp.where(i > 0, prev_ref[...], 0.0)
    hbuf[:, HALO_ROWS:HALO_ROWS + tm, :] = main_ref[...]
    hbuf[:, HALO_ROWS + tm:2 * HALO_ROWS + tm, :] = jnp.where(i < n_tiles - 1, next_ref[...], 0.0)
    base = HALO_ROWS - CONV_PAD

    def conv_chunk(c, total):
        acc = w_ref[c, 0:1, :] * hbuf[c, base:base + tm, :]
        for k in range(1, CONV_WIDTH):
            acc = acc + w_ref[c, k:k + 1, :] * hbuf[c, base + k:base + k + tm, :]
        acc = acc + b_ref[c]
        cbuf[c] = acc
        return total + acc

    ch = nch * LANES
    total = lax.fori_loop(0, nch, conv_chunk, jnp.zeros((tm, LANES), F32))
    mu = jnp.sum(total, axis=-1, keepdims=True) / ch

    def sq_dev(c, total):
        d = cbuf[c] - mu
        return total + d * d

    dev = lax.fori_loop(0, nch, sq_dev, jnp.zeros((tm, LANES), F32))
    inv = lax.rsqrt(jnp.sum(dev, axis=-1, keepdims=True) / ch + NORM_EPS)
    for c in range(nch):
        hn = (cbuf[c] - mu) * inv * g_ref[c] + beta_ref[c]
        o_ref[:, c * LANES:(c + 1) * LANES] = _silu(hn).astype(o_ref.dtype)


def _conv_front(h3, w3, b3, g3, beta3, layer, tm=256):
    nch, m, _ = h3.shape
    n_tiles = m // tm
    per = tm // HALO_ROWS
    n_halo = m // HALO_ROWS
    vec = pl.BlockSpec((nch, 1, LANES), lambda i: (layer, 0, 0))
    return pl.pallas_call(
        functools.partial(_conv_kernel, n_tiles),
        grid=(n_tiles,),
        in_specs=[pl.BlockSpec((nch, HALO_ROWS, LANES),
                               lambda i: (0, jnp.maximum(i * per - 1, 0), 0)),
                  pl.BlockSpec((nch, tm, LANES), lambda i: (0, i, 0)),
                  pl.BlockSpec((nch, HALO_ROWS, LANES),
                               lambda i: (0, jnp.minimum((i + 1) * per, n_halo - 1), 0)),
                  pl.BlockSpec((nch, CONV_WIDTH, LANES), lambda i: (layer, 0, 0)),
                  vec, vec, vec],
        out_specs=pl.BlockSpec((tm, nch * LANES), lambda i: (i, 0)),
        out_shape=jax.ShapeDtypeStruct((m, nch * LANES), BF16),
        scratch_shapes=[pltpu.VMEM((nch, tm + 2 * HALO_ROWS, LANES), F32),
                        pltpu.VMEM((nch, tm, LANES), F32)],
        compiler_params=_params(1),
        name="conv_front",
    )(h3, h3, h3, w3, b3, g3, beta3)


def _pw_kernel(x_ref, w_ref, b_ref, g_ref, o_ref):
    y = _bdot(x_ref[...], w_ref[...].astype(BF16)) + b_ref[...]
    ms = jnp.mean(y * y, axis=-1, keepdims=True)
    o_ref[...] = (y * lax.rsqrt(ms + NORM_EPS) * g_ref[...]).astype(o_ref.dtype)


def _pointwise(x, w_pw, b3, g3, layer, bm=512):
    m, k = x.shape
    n = w_pw.shape[2]
    vec = pl.BlockSpec((None, 1, n), lambda i: (layer, 0, 0))
    return pl.pallas_call(
        _pw_kernel,
        grid=(m // bm,),
        in_specs=[pl.BlockSpec((bm, k), lambda i: (i, 0)),
                  pl.BlockSpec((None, k, n), lambda i: (layer, 0, 0)),
                  vec, vec],
        out_specs=pl.BlockSpec((bm, n), lambda i: (i, 0)),
        out_shape=jax.ShapeDtypeStruct((m, n), BF16),
        compiler_params=_params(1),
        name="pointwise",
    )(x, w_pw, b3, g3)


def _chunk_major(v):
    lead, c = v.shape[0], v.shape[-1]
    mid = v.shape[1:-1]
    v = v.reshape(lead, -1, c // LANES, LANES)
    v = jnp.swapaxes(v, 1, 2)
    return v.reshape(lead * (c // LANES), *(mid or (1,)), LANES)


def kernel(x, positions, ffn1_norm, ffn1_w_gate, ffn1_w_up, ffn1_w_down, mix_norm, w_in,
           attn_sink, conv_w_dw, conv_b_dw, conv_ln_g, conv_ln_b, conv_w_pw, conv_b_pw,
           attn_out_gain, conv_out_gain, w_out, ffn2_norm, ffn2_w_gate, ffn2_w_up,
           ffn2_w_down, final_norm):
    b, s, d = x.shape
    depth = w_in.shape[0]
    d_ff = ffn1_w_gate.shape[2]
    d_attn = attn_out_gain.shape[1]
    d_conv = conv_out_gain.shape[1]
    assert b == 1 and positions.shape == (b, s)
    assert w_in.shape[2] == d_attn + 2 * D_KV + 2 * d_conv

    as3 = lambda v: v.reshape(v.shape[0], 1, v.shape[1])
    ffn1_norm3, ffn2_norm3, mix_norm3 = as3(ffn1_norm), as3(ffn2_norm), as3(mix_norm)
    b_pw3, attn_gain3, conv_gain3 = as3(conv_b_pw), as3(attn_out_gain), as3(conv_out_gain)
    w_dw_c, b_dw_c = _chunk_major(conv_w_dw), _chunk_major(conv_b_dw)
    ln_g_c, ln_b_c = _chunk_major(conv_ln_g), _chunk_major(conv_ln_b)

    bn_ffn, bn_glu, bn_qkv, bn_down, bn_out = 512, 256, 512, 256, 512
    tables = _rope_tables(positions)
    h = x.reshape(s, d)
    hg, ssq = _norm_prep(h, ffn1_norm3, 0)

    def ffn(h, hg, ssq, wg, wu, wd, l, next_gain):
        act = _gated_mm(hg, ssq, wg, wu, l, 0, 0, d_ff, _swiglu_act, BF16, bn_ffn,
                        lhs_buffers=1)
        return _resid_mm([act], wd, l, h, FFN_RESID, next_gain, bn_down)

    for l in range(depth):
        h, hg, ssq = ffn(h, hg, ssq, ffn1_w_gate, ffn1_w_up, ffn1_w_down, l, (mix_norm3, l))
        qkv = _qkv_proj(hg, ssq, w_in, l, tables, d_attn, bn_qkv)
        u_off = (d_attn + 2 * D_KV) // bn_glu
        glu = _gated_mm(hg, ssq, w_in, w_in, l, u_off, u_off + d_conv // bn_glu, d_conv,
                        _glu_act, F32, bn_glu, chunk_major=True)
        attn = _attention(qkv, attn_sink, attn_gain3, l, d_attn)
        conv = _conv_front(glu, w_dw_c, b_dw_c, ln_g_c, ln_b_c, l)
        conv = _pointwise(conv, conv_w_pw, b_pw3, conv_gain3, l)
        h, hg, ssq = _resid_mm([attn, conv], w_out, l, h, 1.0, (ffn2_norm3, l), bn_out)
        if l + 1 < depth:
            h, hg, ssq = ffn(h, hg, ssq, ffn2_w_gate, ffn2_w_up, ffn2_w_down, l,
                             (ffn1_norm3, l + 1))
        else:
            h = ffn(h, hg, ssq, ffn2_w_gate, ffn2_w_up, ffn2_w_down, l, None)

    out = _rmsnorm(h, final_norm.reshape(1, 1, d), 0, F32)
    return out.reshape(b, s, d)
```

```python
import functools
import math

import jax
import jax.numpy as jnp
import numpy as np
from jax import lax
from jax.experimental import pallas as pl
from jax.experimental.pallas import tpu as pltpu

F32 = jnp.float32
BF16 = jnp.bfloat16

LANES = 128
HEAD_DIM = 128
N_KV_HEADS = 4
GQA_GROUP = 4
D_KV = N_KV_HEADS * HEAD_DIM
CONV_WIDTH = 31
CONV_PAD = CONV_WIDTH // 2
ATTN_BLOCK = 128
ROPE_DIM = HEAD_DIM // 4
ROPE_THETA = 500000.0
NORM_EPS = 1e-6
FFN_RESID = 0.5
MASK_VALUE = -1e30
LOG2E = math.log2(math.e)

VMEM_LIMIT_BYTES = 58 * 1024 * 1024
HALO_ROWS = 16


def _params(n_axes):
    return pltpu.CompilerParams(
        dimension_semantics=("arbitrary",) * n_axes,
        vmem_limit_bytes=VMEM_LIMIT_BYTES)


def _sigmoid(x):
    return 1.0 / (1.0 + jnp.exp(-x))


def _silu(x):
    return x * _sigmoid(x)


def _bdot(a, b):
    return jnp.dot(a, b, preferred_element_type=F32)


def _lane_fold(x):
    acc = x[:, :LANES]
    for c in range(1, x.shape[1] // LANES):
        acc = acc + x[:, c * LANES:(c + 1) * LANES]
    return acc


def _row_rstd(ssq_ref, d):
    return lax.rsqrt(jnp.sum(ssq_ref[...], axis=-1, keepdims=True) / d + NORM_EPS)


def _norm_prep_kernel(x_ref, g_ref, hg_ref, ssq_ref):
    x = x_ref[...]
    hg_ref[...] = (x * g_ref[...]).astype(hg_ref.dtype)
    ssq_ref[...] = _lane_fold(x * x)


def _norm_prep(x, gain3, layer, rows=256):
    m, d = x.shape
    return pl.pallas_call(
        _norm_prep_kernel,
        grid=(m // rows,),
        in_specs=[pl.BlockSpec((rows, d), lambda i: (i, 0)),
                  pl.BlockSpec((None, 1, d), lambda i: (layer, 0, 0))],
        out_specs=[pl.BlockSpec((rows, d), lambda i: (i, 0)),
                   pl.BlockSpec((rows, LANES), lambda i: (i, 0))],
        out_shape=[jax.ShapeDtypeStruct((m, d), BF16),
                   jax.ShapeDtypeStruct((m, LANES), F32)],
        compiler_params=_params(1),
        name="norm_prep",
    )(x, gain3)


def _rmsnorm_kernel(x_ref, g_ref, o_ref):
    x = x_ref[...]
    ms = jnp.mean(x * x, axis=-1, keepdims=True)
    o_ref[...] = (x * lax.rsqrt(ms + NORM_EPS) * g_ref[...]).astype(o_ref.dtype)


def _rmsnorm(x, gain3, layer, out_dtype, rows=256):
    m, d = x.shape
    return pl.pallas_call(
        _rmsnorm_kernel,
        grid=(m // rows,),
        in_specs=[pl.BlockSpec((rows, d), lambda i: (i, 0)),
                  pl.BlockSpec((None, 1, d), lambda i: (layer, 0, 0))],
        out_specs=pl.BlockSpec((rows, d), lambda i: (i, 0)),
        out_shape=jax.ShapeDtypeStruct((m, d), out_dtype),
        compiler_params=_params(1),
        name="rmsnorm",
    )(x, gain3)


def _gated_mm_kernel(act, chunk_major, n_slabs, x_ref, ssq_ref, wa_ref, wb_ref, o_ref):
    wa = wa_ref[...].astype(BF16)
    wb = wb_ref[...].astype(BF16)
    bm, k = x_ref.shape
    sm = bm // n_slabs
    for s in range(n_slabs):
        rows = slice(s * sm, (s + 1) * sm)
        x = x_ref[rows, :]
        rstd = _row_rstd(ssq_ref.at[rows, :], k)
        y = act(_bdot(x, wa) * rstd, _bdot(x, wb) * rstd).astype(o_ref.dtype)
        if chunk_major:
            for c in range(y.shape[1] // LANES):
                o_ref[c, rows, :] = y[:, c * LANES:(c + 1) * LANES]
        else:
            o_ref[rows, :] = y


def _gated_mm(hg, ssq, wa, wb, layer, col_a, col_b, n_out, act, out_dtype, bm, bn, n_slabs,
              chunk_major=False):
    m, k = hg.shape
    lhs_mode = pl.Buffered(1)
    if chunk_major:
        out_spec = pl.BlockSpec((bn // LANES, bm, LANES), lambda i, j: (j, i, 0))
        out_shape = jax.ShapeDtypeStruct((n_out // LANES, m, LANES), out_dtype)
    else:
        out_spec = pl.BlockSpec((bm, bn), lambda i, j: (i, j))
        out_shape = jax.ShapeDtypeStruct((m, n_out), out_dtype)
    return pl.pallas_call(
        functools.partial(_gated_mm_kernel, act, chunk_major, n_slabs),
        grid=(m // bm, n_out // bn),
        in_specs=[pl.BlockSpec((bm, k), lambda i, j: (i, 0), pipeline_mode=lhs_mode),
                  pl.BlockSpec((bm, LANES), lambda i, j: (i, 0)),
                  pl.BlockSpec((None, k, bn), lambda i, j: (layer, 0, j + col_a)),
                  pl.BlockSpec((None, k, bn), lambda i, j: (layer, 0, j + col_b))],
        out_specs=out_spec,
        out_shape=out_shape,
        compiler_params=_params(2),
        name="gated_mm",
    )(hg, ssq, wa, wb)


def _swiglu_act(g, u):
    return _silu(g) * u


def _glu_act(a, gate):
    return a * _sigmoid(gate)


def _resid_mm_kernel(n_lhs, scale, emit_norm, *refs):
    lhs_refs = refs[:n_lhs]
    w_refs = refs[n_lhs:2 * n_lhs]
    r_ref = refs[2 * n_lhs]
    acc = _bdot(lhs_refs[0][...], w_refs[0][...].astype(BF16))
    for t in range(1, n_lhs):
        acc = acc + _bdot(lhs_refs[t][...], w_refs[t][...].astype(BF16))
    if scale != 1.0:
        acc = scale * acc
    h = r_ref[...] + acc
    if not emit_norm:
        refs[2 * n_lhs + 1][...] = h
        return
    g_ref, o_ref, hg_ref, ssq_ref = refs[2 * n_lhs + 1:]
    j = pl.program_id(1)
    bn = o_ref.shape[1]
    o_ref[...] = h
    g = g_ref[:, pl.ds(pl.multiple_of(j * bn, bn), bn)]
    hg_ref[...] = (h * g).astype(hg_ref.dtype)
    part = _lane_fold(h * h)

    @pl.when(j == 0)
    def _():
        ssq_ref[...] = part

    @pl.when(j > 0)
    def _():
        ssq_ref[...] += part


def _resid_mm(lhs_list, w, layer, resid, scale, next_gain, bn, bm=1024, lhs_buffers=2):
    m, kk = lhs_list[0].shape
    n = resid.shape[1]
    n_lhs = len(lhs_list)
    tile = pl.BlockSpec((bm, bn), lambda i, j: (i, j))
    in_specs = [pl.BlockSpec((bm, kk), lambda i, j: (i, 0),
                             pipeline_mode=pl.Buffered(lhs_buffers)) for _ in lhs_list]
    in_specs += [pl.BlockSpec((None, kk, bn), functools.partial(
        lambda t, i, j: (layer, t, j), t)) for t in range(n_lhs)]
    in_specs += [tile]
    args = [*lhs_list, *([w] * n_lhs), resid]
    h_shape = jax.ShapeDtypeStruct((m, n), F32)
    if next_gain is None:
        out_specs, out_shape = tile, h_shape
    else:
        gain3, gl = next_gain
        in_specs += [pl.BlockSpec((None, 1, n), lambda i, j: (gl, 0, 0))]
        args += [gain3]
        out_specs = [tile, tile, pl.BlockSpec((bm, LANES), lambda i, j: (i, 0))]
        out_shape = [h_shape, jax.ShapeDtypeStruct((m, n), BF16),
                     jax.ShapeDtypeStruct((m, LANES), F32)]
    return pl.pallas_call(
        functools.partial(_resid_mm_kernel, n_lhs, scale, next_gain is not None),
        grid=(m // bm, n // bn),
        in_specs=in_specs,
        out_specs=out_specs,
        out_shape=out_shape,
        compiler_params=_params(2),
        name="resid_mm",
    )(*args)


def _rope_table_kernel(pos_ref, freq_ref, c_ref, slo_ref, shi_ref):
    half = ROPE_DIM // 2
    ang = pos_ref[...].astype(F32) * freq_ref[...]
    cos, sin = jnp.cos(ang), jnp.sin(ang)
    lane = lax.broadcasted_iota(jnp.int32, ang.shape, 1)
    c_ref[...] = jnp.where(lane < ROPE_DIM, cos, 1.0)
    slo_ref[...] = jnp.where(lane < half, -sin, 0.0)
    shi_ref[...] = jnp.where((lane >= half) & (lane < ROPE_DIM), sin, 0.0)


def _rope_tables(positions, rows=512):
    m = positions.size
    half = ROPE_DIM // 2
    inv_freq = ROPE_THETA ** (-np.arange(half, dtype=np.float32) * np.float32(2.0) / ROPE_DIM)
    lane_freq = np.zeros((1, HEAD_DIM), np.float32)
    lane_freq[0, :half] = inv_freq
    lane_freq[0, half:ROPE_DIM] = inv_freq
    tab = jax.ShapeDtypeStruct((m, HEAD_DIM), F32)
    spec = pl.BlockSpec((rows, HEAD_DIM), lambda i: (i, 0))
    return pl.pallas_call(
        _rope_table_kernel,
        grid=(m // rows,),
        in_specs=[pl.BlockSpec((rows, 1), lambda i: (i, 0)),
                  pl.BlockSpec((1, HEAD_DIM), lambda i: (0, 0))],
        out_specs=[spec, spec, spec],
        out_shape=[tab, tab, tab],
        compiler_params=_params(1),
        name="rope_tables",
    )(positions.reshape(m, 1), jnp.asarray(lane_freq))


def _qkv_kernel(n_rope_tiles, n_slabs, x_ref, ssq_ref, w_ref, c_ref, slo_ref, shi_ref, o_ref):
    half = ROPE_DIM // 2
    is_rope = (pl.program_id(1) < n_rope_tiles).astype(F32)
    w = w_ref[...].astype(BF16)
    bm, bn = o_ref.shape
    sm = bm // n_slabs
    for s in range(n_slabs):
        rows = slice(s * sm, (s + 1) * sm)
        rstd = _row_rstd(ssq_ref.at[rows, :], x_ref.shape[1])
        y = _bdot(x_ref[rows, :], w) * rstd
        c = 1.0 + is_rope * (c_ref[rows, :] - 1.0)
        slo = is_rope * slo_ref[rows, :]
        shi = is_rope * shi_ref[rows, :]
        for h in range(bn // HEAD_DIM):
            yh = y[:, h * HEAD_DIM:(h + 1) * HEAD_DIM]
            rot = (yh * c + pltpu.roll(yh, HEAD_DIM - half, 1) * slo
                   + pltpu.roll(yh, half, 1) * shi)
            o_ref[rows, h * HEAD_DIM:(h + 1) * HEAD_DIM] = rot.astype(o_ref.dtype)


def _qkv_proj(hg, ssq, w_in, layer, tables, d_attn, bn, bm=2048, n_slabs=4):
    m, k = hg.shape
    n_out = d_attn + 2 * D_KV
    n_rope_tiles = (d_attn + D_KV) // bn
    tspec = pl.BlockSpec((bm, HEAD_DIM), lambda i, j: (i, 0))
    return pl.pallas_call(
        functools.partial(_qkv_kernel, n_rope_tiles, n_slabs),
        grid=(m // bm, n_out // bn),
        in_specs=[pl.BlockSpec((bm, k), lambda i, j: (i, 0), pipeline_mode=pl.Buffered(1)),
                  pl.BlockSpec((bm, LANES), lambda i, j: (i, 0)),
                  pl.BlockSpec((None, k, bn), lambda i, j: (layer, 0, j)),
                  tspec, tspec, tspec],
        out_specs=pl.BlockSpec((bm, bn), lambda i, j: (i, j)),
        out_shape=jax.ShapeDtypeStruct((m, n_out), BF16),
        compiler_params=_params(2),
        name="qkv_proj",
    )(hg, ssq, w_in, *tables)


def _attn_kernel(layer, n_blocks, sink_ref, q_ref, kp_ref, kc_ref, kn_ref,
                 vp_ref, vc_ref, vn_ref, g_ref, o_ref, acc_ref):
    n = pl.program_id(0)
    blk = ATTN_BLOCK
    r = lax.broadcasted_iota(jnp.int32, (blk, blk), 0)
    c = lax.broadcasted_iota(jnp.int32, (blk, blk), 1)
    prev_ok = (c >= r) & (n > 0)
    next_ok = (c <= r) & (n < n_blocks - 1)
    scale2 = HEAD_DIM ** -0.5 * LOG2E
    ones = jnp.ones((3 * blk, HEAD_DIM), BF16)
    ss = jnp.zeros((blk, 1), F32)
    for h in range(N_KV_HEADS):
        hs = slice(h * HEAD_DIM, (h + 1) * HEAD_DIM)
        kw = jnp.concatenate([kp_ref[:, hs], kc_ref[:, hs], kn_ref[:, hs]], axis=0)
        vw = jnp.concatenate([vp_ref[:, hs], vc_ref[:, hs], vn_ref[:, hs]], axis=0)
        vw1 = jnp.concatenate([vw, ones], axis=1)
        for g in range(GQA_GROUP):
            qs = slice((h * GQA_GROUP + g) * HEAD_DIM, (h * GQA_GROUP + g + 1) * HEAD_DIM)
            s = lax.dot_general(q_ref[:, qs], kw, (((1,), (1,)), ((), ())),
                                preferred_element_type=F32) * scale2
            s_prev = jnp.where(prev_ok, s[:, :blk], MASK_VALUE)
            s_cur = s[:, blk:2 * blk]
            s_next = jnp.where(next_ok, s[:, 2 * blk:], MASK_VALUE)
            sink2 = sink_ref[layer, h * GQA_GROUP + g] * LOG2E
            mx = jnp.max(jnp.maximum(jnp.maximum(s_prev, s_cur), s_next),
                         axis=-1, keepdims=True)
            mx = jnp.maximum(mx, sink2)
            p = jnp.concatenate([jnp.exp2(s_prev - mx), jnp.exp2(s_cur - mx),
                                 jnp.exp2(s_next - mx)], axis=1).astype(BF16)
            pv = _bdot(p, vw1)
            den = pv[:, HEAD_DIM:] + jnp.exp2(sink2 - mx)
            o = pv[:, :HEAD_DIM] / den
            ss = ss + jnp.sum(o * o, axis=-1, keepdims=True)
            acc_ref[:, qs] = o
    inv = lax.rsqrt(ss / acc_ref.shape[1] + NORM_EPS)
    o_ref[...] = (acc_ref[...] * inv * g_ref[...]).astype(o_ref.dtype)


def _attention(qkv, sink, gain3, layer, d_attn):
    m = qkv.shape[0]
    blk = ATTN_BLOCK
    nb = m // blk
    kcol = d_attn // D_KV
    vcol = kcol + 1
    prev = lambda n: jnp.maximum(n - 1, 0)
    nxt = lambda n: jnp.minimum(n + 1, nb - 1)
    kv = lambda rowf, col: pl.BlockSpec((blk, D_KV), lambda n: (rowf(n), col))
    same = lambda n: n
    return pl.pallas_call(
        functools.partial(_attn_kernel, layer, nb),
        grid=(nb,),
        in_specs=[pl.BlockSpec(memory_space=pltpu.SMEM),
                  pl.BlockSpec((blk, d_attn), lambda n: (n, 0)),
                  kv(prev, kcol), kv(same, kcol), kv(nxt, kcol),
                  kv(prev, vcol), kv(same, vcol), kv(nxt, vcol),
                  pl.BlockSpec((None, 1, d_attn), lambda n: (layer, 0, 0))],
        out_specs=pl.BlockSpec((blk, d_attn), lambda n: (n, 0)),
        out_shape=jax.ShapeDtypeStruct((m, d_attn), BF16),
        scratch_shapes=[pltpu.VMEM((blk, d_attn), F32)],
        compiler_params=_params(1),
        name="window_attn",
    )(sink, qkv, qkv, qkv, qkv, qkv, qkv, qkv, gain3)


def _conv_kernel(n_tiles, prev_ref, main_ref, next_ref, w_ref, b_ref,
                 g_ref, beta_ref, o_ref, hbuf, cbuf):
    i = pl.program_id(0)
    nch, tm, _ = main_ref.shape
    hbuf[:, 0:HALO_ROWS, :] = jnp.where(i > 0, prev_ref[...], 0.0)
    hbuf[:, HALO_ROWS:HALO_ROWS + tm, :] = main_ref[...]
    hbuf[:, HALO_ROWS + tm:2 * HALO_ROWS + tm, :] = jnp.where(i < n_tiles - 1, next_ref[...], 0.0)
    base = HALO_ROWS - CONV_PAD

    def conv_chunk(c, total):
        acc = w_ref[c, 0:1, :] * hbuf[c, base:base + tm, :]
        for k in range(1, CONV_WIDTH):
            acc = acc + w_ref[c, k:k + 1, :] * hbuf[c, base + k:base + k + tm, :]
        acc = acc + b_ref[c]
        cbuf[c] = acc
        return total + acc

    ch = nch * LANES
    total = lax.fori_loop(0, nch, conv_chunk, jnp.zeros((tm, LANES), F32))
    mu = jnp.sum(total, axis=-1, keepdims=True) / ch

    def sq_dev(c, total):
        d = cbuf[c] - mu
        return total + d * d

    dev = lax.fori_loop(0, nch, sq_dev, jnp.zeros((tm, LANES), F32))
    inv = lax.rsqrt(jnp.sum(dev, axis=-1, keepdims=True) / ch + NORM_EPS)
    for c in range(nch):
        hn = (cbuf[c] - mu) * inv * g_ref[c] + beta_ref[c]
        o_ref[:, c * LANES:(c + 1) * LANES] = _silu(hn).astype(o_ref.dtype)


def _conv_front(h3, w3, b3, g3, beta3, layer, tm=256):
    nch, m, _ = h3.shape
    n_tiles = m // tm
    per = tm // HALO_ROWS
    n_halo = m // HALO_ROWS
    vec = pl.BlockSpec((nch, 1, LANES), lambda i: (layer, 0, 0))
    return pl.pallas_call(
        functools.partial(_conv_kernel, n_tiles),
        grid=(n_tiles,),
        in_specs=[pl.BlockSpec((nch, HALO_ROWS, LANES),
                               lambda i: (0, jnp.maximum(i * per - 1, 0), 0)),
                  pl.BlockSpec((nch, tm, LANES), lambda i: (0, i, 0)),
                  pl.BlockSpec((nch, HALO_ROWS, LANES),
                               lambda i: (0, jnp.minimum((i + 1) * per, n_halo - 1), 0)),
                  pl.BlockSpec((nch, CONV_WIDTH, LANES), lambda i: (layer, 0, 0)),
                  vec, vec, vec],
        out_specs=pl.BlockSpec((tm, nch * LANES), lambda i: (i, 0)),
        out_shape=jax.ShapeDtypeStruct((m, nch * LANES), BF16),
        scratch_shapes=[pltpu.VMEM((nch, tm + 2 * HALO_ROWS, LANES), F32),
                        pltpu.VMEM((nch, tm, LANES), F32)],
        compiler_params=_params(1),
        name="conv_front",
    )(h3, h3, h3, w3, b3, g3, beta3)


def _pw_kernel(x_ref, w_ref, b_ref, g_ref, o_ref):
    y = _bdot(x_ref[...], w_ref[...].astype(BF16)) + b_ref[...]
    ms = jnp.mean(y * y, axis=-1, keepdims=True)
    o_ref[...] = (y * lax.rsqrt(ms + NORM_EPS) * g_ref[...]).astype(o_ref.dtype)


def _pointwise(x, w_pw, b3, g3, layer, bm=512):
    m, k = x.shape
    n = w_pw.shape[2]
    vec = pl.BlockSpec((None, 1, n), lambda i: (layer, 0, 0))
    return pl.pallas_call(
        _pw_kernel,
        grid=(m // bm,),
        in_specs=[pl.BlockSpec((bm, k), lambda i: (i, 0)),
                  pl.BlockSpec((None, k, n), lambda i: (layer, 0, 0)),
                  vec, vec],
        out_specs=pl.BlockSpec((bm, n), lambda i: (i, 0)),
        out_shape=jax.ShapeDtypeStruct((m, n), BF16),
        compiler_params=_params(1),
        name="pointwise",
    )(x, w_pw, b3, g3)


def _chunk_major(v):
    lead, c = v.shape[0], v.shape[-1]
    mid = v.shape[1:-1]
    v = v.reshape(lead, -1, c // LANES, LANES)
    v = jnp.swapaxes(v, 1, 2)
    return v.reshape(lead * (c // LANES), *(mid or (1,)), LANES)


def kernel(x, positions, ffn1_norm, ffn1_w_gate, ffn1_w_up, ffn1_w_down, mix_norm, w_in,
           attn_sink, conv_w_dw, conv_b_dw, conv_ln_g, conv_ln_b, conv_w_pw, conv_b_pw,
           attn_out_gain, conv_out_gain, w_out, ffn2_norm, ffn2_w_gate, ffn2_w_up,
           ffn2_w_down, final_norm):
    b, s, d = x.shape
    depth = w_in.shape[0]
    d_ff = ffn1_w_gate.shape[2]
    d_attn = attn_out_gain.shape[1]
    d_conv = conv_out_gain.shape[1]
    assert b == 1 and positions.shape == (b, s)
    assert w_in.shape[2] == d_attn + 2 * D_KV + 2 * d_conv

    as3 = lambda v: v.reshape(v.shape[0], 1, v.shape[1])
    ffn1_norm3, ffn2_norm3, mix_norm3 = as3(ffn1_norm), as3(ffn2_norm), as3(mix_norm)
    b_pw3, attn_gain3, conv_gain3 = as3(conv_b_pw), as3(attn_out_gain), as3(conv_out_gain)
    w_dw_c, b_dw_c = _chunk_major(conv_w_dw), _chunk_major(conv_b_dw)
    ln_g_c, ln_b_c = _chunk_major(conv_ln_g), _chunk_major(conv_ln_b)

    bn_glu, bn_qkv, bn_down, bn_out = 256, 512, 512, 512
    tables = _rope_tables(positions)
    h = x.reshape(s, d)
    hg, ssq = _norm_prep(h, ffn1_norm3, 0)

    def ffn(h, hg, ssq, wg, wu, wd, l, next_gain):
        act = _gated_mm(hg, ssq, wg, wu, l, 0, 0, d_ff, _swiglu_act, BF16,
                        bm=1024, bn=512, n_slabs=1)
        return _resid_mm([act], wd, l, h, FFN_RESID, next_gain, bn_down, lhs_buffers=1)

    for l in range(depth):
        h, hg, ssq = ffn(h, hg, ssq, ffn1_w_gate, ffn1_w_up, ffn1_w_down, l, (mix_norm3, l))
        qkv = _qkv_proj(hg, ssq, w_in, l, tables, d_attn, bn_qkv)
        u_off = (d_attn + 2 * D_KV) // bn_glu
        glu = _gated_mm(hg, ssq, w_in, w_in, l, u_off, u_off + d_conv // bn_glu, d_conv,
                        _glu_act, F32, bm=2048, bn=bn_glu, n_slabs=2, chunk_major=True)
        attn = _attention(qkv, attn_sink, attn_gain3, l, d_attn)
        conv = _conv_front(glu, w_dw_c, b_dw_c, ln_g_c, ln_b_c, l)
        conv = _pointwise(conv, conv_w_pw, b_pw3, conv_gain3, l)
        h, hg, ssq = _resid_mm([attn, conv], w_out, l, h, 1.0, (ffn2_norm3, l), bn_out)
        if l + 1 < depth:
            h, hg, ssq = ffn(h, hg, ssq, ffn2_w_gate, ffn2_w_up, ffn2_w_down, l,
                             (ffn1_norm3, l + 1))
        else:
            h = ffn(h, hg, ssq, ffn2_w_gate, ffn2_w_up, ffn2_w_down, l, None)

    out = _rmsnorm(h, final_norm.reshape(1, 1, d), 0, F32)
    return out.reshape(b, s, d)
```

```python
import functools
import math

import jax
import jax.numpy as jnp
import numpy as np
from jax import lax
from jax.experimental import pallas as pl
from jax.experimental.pallas import tpu as pltpu

F32 = jnp.float32
BF16 = jnp.bfloat16

LANES = 128
HEAD_DIM = 128
N_KV_HEADS = 4
GQA_GROUP = 4
D_KV = N_KV_HEADS * HEAD_DIM
CONV_WIDTH = 31
CONV_PAD = CONV_WIDTH // 2
ATTN_BLOCK = 128
ROPE_DIM = HEAD_DIM // 4
ROPE_THETA = 500000.0
NORM_EPS = 1e-6
FFN_RESID = 0.5
MASK_VALUE = -1e30
LOG2E = math.log2(math.e)

VMEM_LIMIT_BYTES = 62 * 1024 * 1024
HALO_ROWS = 16


def _params(n_axes):
    return pltpu.CompilerParams(
        dimension_semantics=("arbitrary",) * n_axes,
        vmem_limit_bytes=VMEM_LIMIT_BYTES)


def _sigmoid(x):
    return 1.0 / (1.0 + jnp.exp(-x))


def _silu(x):
    return x * _sigmoid(x)


def _bdot(a, b):
    return jnp.dot(a, b, preferred_element_type=F32)


def _lane_fold(x):
    acc = x[:, :LANES]
    for c in range(1, x.shape[1] // LANES):
        acc = acc + x[:, c * LANES:(c + 1) * LANES]
    return acc


def _row_rstd(ssq_ref, d):
    return lax.rsqrt(jnp.sum(ssq_ref[...], axis=-1, keepdims=True) / d + NORM_EPS)


def _norm_prep_kernel(x_ref, g_ref, hg_ref, ssq_ref):
    x = x_ref[...]
    hg_ref[...] = (x * g_ref[...]).astype(hg_ref.dtype)
    ssq_ref[...] = _lane_fold(x * x)


def _norm_prep(x, gain3, layer, rows=256):
    m, d = x.shape
    return pl.pallas_call(
        _norm_prep_kernel,
        grid=(m // rows,),
        in_specs=[pl.BlockSpec((rows, d), lambda i: (i, 0)),
                  pl.BlockSpec((None, 1, d), lambda i: (layer, 0, 0))],
        out_specs=[pl.BlockSpec((rows, d), lambda i: (i, 0)),
                   pl.BlockSpec((rows, LANES), lambda i: (i, 0))],
        out_shape=[jax.ShapeDtypeStruct((m, d), BF16),
                   jax.ShapeDtypeStruct((m, LANES), F32)],
        compiler_params=_params(1),
        name="norm_prep",
    )(x, gain3)


def _rmsnorm_kernel(x_ref, g_ref, o_ref):
    x = x_ref[...]
    ms = jnp.mean(x * x, axis=-1, keepdims=True)
    o_ref[...] = (x * lax.rsqrt(ms + NORM_EPS) * g_ref[...]).astype(o_ref.dtype)


def _rmsnorm(x, gain3, layer, out_dtype, rows=256):
    m, d = x.shape
    return pl.pallas_call(
        _rmsnorm_kernel,
        grid=(m // rows,),
        in_specs=[pl.BlockSpec((rows, d), lambda i: (i, 0)),
                  pl.BlockSpec((None, 1, d), lambda i: (layer, 0, 0))],
        out_specs=pl.BlockSpec((rows, d), lambda i: (i, 0)),
        out_shape=jax.ShapeDtypeStruct((m, d), out_dtype),
        compiler_params=_params(1),
        name="rmsnorm",
    )(x, gain3)


def _gated_mm_kernel(act, chunk_major, n_slabs, x_ref, ssq_ref, wa_ref, wb_ref, o_ref):
    wa = wa_ref[...].astype(BF16)
    wb = wb_ref[...].astype(BF16)
    bm, k = x_ref.shape
    sm = bm // n_slabs
    for s in range(n_slabs):
        rows = slice(s * sm, (s + 1) * sm)
        x = x_ref[rows, :]
        rstd = _row_rstd(ssq_ref.at[rows, :], k)
        y = act(_bdot(x, wa) * rstd, _bdot(x, wb) * rstd).astype(o_ref.dtype)
        if chunk_major:
            for c in range(y.shape[1] // LANES):
                o_ref[c, rows, :] = y[:, c * LANES:(c + 1) * LANES]
        else:
            o_ref[rows, :] = y


def _gated_mm(hg, ssq, wa, wb, layer, col_a, col_b, n_out, act, out_dtype, bm, bn, n_slabs,
              chunk_major=False):
    m, k = hg.shape
    if chunk_major:
        out_spec = pl.BlockSpec((bn // LANES, bm, LANES), lambda i, j: (j, i, 0))
        out_shape = jax.ShapeDtypeStruct((n_out // LANES, m, LANES), out_dtype)
    else:
        out_spec = pl.BlockSpec((bm, bn), lambda i, j: (i, j))
        out_shape = jax.ShapeDtypeStruct((m, n_out), out_dtype)
    return pl.pallas_call(
        functools.partial(_gated_mm_kernel, act, chunk_major, n_slabs),
        grid=(m // bm, n_out // bn),
        in_specs=[pl.BlockSpec((bm, k), lambda i, j: (i, 0)),
                  pl.BlockSpec((bm, LANES), lambda i, j: (i, 0)),
                  pl.BlockSpec((None, k, bn), lambda i, j: (layer, 0, j + col_a)),
                  pl.BlockSpec((None, k, bn), lambda i, j: (layer, 0, j + col_b))],
        out_specs=out_spec,
        out_shape=out_shape,
        compiler_params=_params(2),
        name="gated_mm",
    )(hg, ssq, wa, wb)


def _swiglu_act(g, u):
    return _silu(g) * u


def _glu_act(a, gate):
    return a * _sigmoid(gate)


def _resid_mm_kernel(n_lhs, scale, emit_norm, *refs):
    lhs_refs = refs[:n_lhs]
    w_refs = refs[n_lhs:2 * n_lhs]
    r_ref = refs[2 * n_lhs]
    acc = _bdot(lhs_refs[0][...], w_refs[0][...].astype(BF16))
    for t in range(1, n_lhs):
        acc = acc + _bdot(lhs_refs[t][...], w_refs[t][...].astype(BF16))
    if scale != 1.0:
        acc = scale * acc
    h = r_ref[...] + acc
    if not emit_norm:
        refs[2 * n_lhs + 1][...] = h
        return
    g_ref, o_ref, hg_ref, ssq_ref = refs[2 * n_lhs + 1:]
    j = pl.program_id(1)
    bn = o_ref.shape[1]
    o_ref[...] = h
    g = g_ref[:, pl.ds(pl.multiple_of(j * bn, bn), bn)]
    hg_ref[...] = (h * g).astype(hg_ref.dtype)
    part = _lane_fold(h * h)

    @pl.when(j == 0)
    def _():
        ssq_ref[...] = part

    @pl.when(j > 0)
    def _():
        ssq_ref[...] += part


def _resid_mm(lhs_list, w, layer, resid, scale, next_gain, bm, bn):
    m, kk = lhs_list[0].shape
    n = resid.shape[1]
    n_lhs = len(lhs_list)
    tile = pl.BlockSpec((bm, bn), lambda i, j: (i, j))
    in_specs = [pl.BlockSpec((bm, kk), lambda i, j: (i, 0)) for _ in lhs_list]
    in_specs += [pl.BlockSpec((None, kk, bn), functools.partial(
        lambda t, i, j: (layer, t, j), t)) for t in range(n_lhs)]
    in_specs += [tile]
    args = [*lhs_list, *([w] * n_lhs), resid]
    h_shape = jax.ShapeDtypeStruct((m, n), F32)
    if next_gain is None:
        out_specs, out_shape = tile, h_shape
    else:
        gain3, gl = next_gain
        in_specs += [pl.BlockSpec((None, 1, n), lambda i, j: (gl, 0, 0))]
        args += [gain3]
        out_specs = [tile, tile, pl.BlockSpec((bm, LANES), lambda i, j: (i, 0))]
        out_shape = [h_shape, jax.ShapeDtypeStruct((m, n), BF16),
                     jax.ShapeDtypeStruct((m, LANES), F32)]
    return pl.pallas_call(
        functools.partial(_resid_mm_kernel, n_lhs, scale, next_gain is not None),
        grid=(m // bm, n // bn),
        in_specs=in_specs,
        out_specs=out_specs,
        out_shape=out_shape,
        compiler_params=_params(2),
        name="resid_mm",
    )(*args)


def _rope_table_kernel(pos_ref, freq_ref, c_ref, slo_ref, shi_ref):
    half = ROPE_DIM // 2
    ang = pos_ref[...].astype(F32) * freq_ref[...]
    cos, sin = jnp.cos(ang), jnp.sin(ang)
    lane = lax.broadcasted_iota(jnp.int32, ang.shape, 1)
    c_ref[...] = jnp.where(lane < ROPE_DIM, cos, 1.0)
    slo_ref[...] = jnp.where(lane < half, -sin, 0.0)
    shi_ref[...] = jnp.where((lane >= half) & (lane < ROPE_DIM), sin, 0.0)


def _rope_tables(positions, rows=512):
    m = positions.size
    half = ROPE_DIM // 2
    inv_freq = ROPE_THETA ** (-np.arange(half, dtype=np.float32) * np.float32(2.0) / ROPE_DIM)
    lane_freq = np.zeros((1, HEAD_DIM), np.float32)
    lane_freq[0, :half] = inv_freq
    lane_freq[0, half:ROPE_DIM] = inv_freq
    tab = jax.ShapeDtypeStruct((m, HEAD_DIM), F32)
    spec = pl.BlockSpec((rows, HEAD_DIM), lambda i: (i, 0))
    return pl.pallas_call(
        _rope_table_kernel,
        grid=(m // rows,),
        in_specs=[pl.BlockSpec((rows, 1), lambda i: (i, 0)),
                  pl.BlockSpec((1, HEAD_DIM), lambda i: (0, 0))],
        out_specs=[spec, spec, spec],
        out_shape=[tab, tab, tab],
        compiler_params=_params(1),
        name="rope_tables",
    )(positions.reshape(m, 1), jnp.asarray(lane_freq))


def _qkv_kernel(n_rope_tiles, n_slabs, x_ref, ssq_ref, w_ref, c_ref, slo_ref, shi_ref, o_ref):
    half = ROPE_DIM // 2
    is_rope = (pl.program_id(1) < n_rope_tiles).astype(F32)
    w = w_ref[...].astype(BF16)
    bm, bn = o_ref.shape
    sm = bm // n_slabs
    for s in range(n_slabs):
        rows = slice(s * sm, (s + 1) * sm)
        rstd = _row_rstd(ssq_ref.at[rows, :], x_ref.shape[1])
        y = _bdot(x_ref[rows, :], w) * rstd
        c = 1.0 + is_rope * (c_ref[rows, :] - 1.0)
        slo = is_rope * slo_ref[rows, :]
        shi = is_rope * shi_ref[rows, :]
        for h in range(bn // HEAD_DIM):
            yh = y[:, h * HEAD_DIM:(h + 1) * HEAD_DIM]
            rot = (yh * c + pltpu.roll(yh, HEAD_DIM - half, 1) * slo
                   + pltpu.roll(yh, half, 1) * shi)
            o_ref[rows, h * HEAD_DIM:(h + 1) * HEAD_DIM] = rot.astype(o_ref.dtype)


def _qkv_proj(hg, ssq, w_in, layer, tables, d_attn, bm, bn, n_slabs, lhs_buffers):
    m, k = hg.shape
    n_out = d_attn + 2 * D_KV
    n_rope_tiles = (d_attn + D_KV) // bn
    tspec = pl.BlockSpec((bm, HEAD_DIM), lambda i, j: (i, 0))
    return pl.pallas_call(
        functools.partial(_qkv_kernel, n_rope_tiles, n_slabs),
        grid=(m // bm, n_out // bn),
        in_specs=[pl.BlockSpec((bm, k), lambda i, j: (i, 0),
                               pipeline_mode=pl.Buffered(lhs_buffers)),
                  pl.BlockSpec((bm, LANES), lambda i, j: (i, 0)),
                  pl.BlockSpec((None, k, bn), lambda i, j: (layer, 0, j)),
                  tspec, tspec, tspec],
        out_specs=pl.BlockSpec((bm, bn), lambda i, j: (i, j)),
        out_shape=jax.ShapeDtypeStruct((m, n_out), BF16),
        compiler_params=_params(2),
        name="qkv_proj",
    )(hg, ssq, w_in, *tables)


def _attn_kernel(layer, n_blocks, sink_ref, q_ref, kp_ref, kc_ref, kn_ref,
                 vp_ref, vc_ref, vn_ref, g_ref, o_ref, acc_ref):
    n = pl.program_id(0)
    blk = ATTN_BLOCK
    r = lax.broadcasted_iota(jnp.int32, (blk, blk), 0)
    c = lax.broadcasted_iota(jnp.int32, (blk, blk), 1)
    prev_ok = (c >= r) & (n > 0)
    next_ok = (c <= r) & (n < n_blocks - 1)
    scale2 = HEAD_DIM ** -0.5 * LOG2E
    ones = jnp.ones((3 * blk, HEAD_DIM), BF16)
    ss = jnp.zeros((blk, 1), F32)
    for h in range(N_KV_HEADS):
        hs = slice(h * HEAD_DIM, (h + 1) * HEAD_DIM)
        kw = jnp.concatenate([kp_ref[:, hs], kc_ref[:, hs], kn_ref[:, hs]], axis=0)
        vw = jnp.concatenate([vp_ref[:, hs], vc_ref[:, hs], vn_ref[:, hs]], axis=0)
        vw1 = jnp.concatenate([vw, ones], axis=1)
        for g in range(GQA_GROUP):
            qs = slice((h * GQA_GROUP + g) * HEAD_DIM, (h * GQA_GROUP + g + 1) * HEAD_DIM)
            s = lax.dot_general(q_ref[:, qs], kw, (((1,), (1,)), ((), ())),
                                preferred_element_type=F32) * scale2
            s_prev = jnp.where(prev_ok, s[:, :blk], MASK_VALUE)
            s_cur = s[:, blk:2 * blk]
            s_next = jnp.where(next_ok, s[:, 2 * blk:], MASK_VALUE)
            sink2 = sink_ref[layer, h * GQA_GROUP + g] * LOG2E
            mx = jnp.max(jnp.maximum(jnp.maximum(s_prev, s_cur), s_next),
                         axis=-1, keepdims=True)
            mx = jnp.maximum(mx, sink2)
            p = jnp.concatenate([jnp.exp2(s_prev - mx), jnp.exp2(s_cur - mx),
                                 jnp.exp2(s_next - mx)], axis=1).astype(BF16)
            pv = _bdot(p, vw1)
            den = pv[:, HEAD_DIM:] + jnp.exp2(sink2 - mx)
            o = pv[:, :HEAD_DIM] / den
            ss = ss + jnp.sum(o * o, axis=-1, keepdims=True)
            acc_ref[:, qs] = o
    inv = lax.rsqrt(ss / acc_ref.shape[1] + NORM_EPS)
    o_ref[...] = (acc_ref[...] * inv * g_ref[...]).astype(o_ref.dtype)


def _attention(qkv, sink, gain3, layer, d_attn):
    m = qkv.shape[0]
    blk = ATTN_BLOCK
    nb = m // blk
    kcol = d_attn // D_KV
    vcol = kcol + 1
    prev = lambda n: jnp.maximum(n - 1, 0)
    nxt = lambda n: jnp.minimum(n + 1, nb - 1)
    kv = lambda rowf, col: pl.BlockSpec((blk, D_KV), lambda n: (rowf(n), col))
    same = lambda n: n
    return pl.pallas_call(
        functools.partial(_attn_kernel, layer, nb),
        grid=(nb,),
        in_specs=[pl.BlockSpec(memory_space=pltpu.SMEM),
                  pl.BlockSpec((blk, d_attn), lambda n: (n, 0)),
                  kv(prev, kcol), kv(same, kcol), kv(nxt, kcol),
                  kv(prev, vcol), kv(same, vcol), kv(nxt, vcol),
                  pl.BlockSpec((None, 1, d_attn), lambda n: (layer, 0, 0))],
        out_specs=pl.BlockSpec((blk, d_attn), lambda n: (n, 0)),
        out_shape=jax.ShapeDtypeStruct((m, d_attn), BF16),
        scratch_shapes=[pltpu.VMEM((blk, d_attn), F32)],
        compiler_params=_params(1),
        name="window_attn",
    )(sink, qkv, qkv, qkv, qkv, qkv, qkv, qkv, gain3)


def _conv_kernel(n_tiles, prev_ref, main_ref, next_ref, w_ref, b_ref,
                 g_ref, beta_ref, o_ref, hbuf, cbuf):
    i = pl.program_id(0)
    nch, tm, _ = main_ref.shape
    hbuf[:, 0:HALO_ROWS, :] = jnp.where(i > 0, prev_ref[...], 0.0)
    hbuf[:, HALO_ROWS:HALO_ROWS + tm, :] = main_ref[...]
    hbuf[:, HALO_ROWS + tm:2 * HALO_ROWS + tm, :] = jnp.where(i < n_tiles - 1, next_ref[...], 0.0)
    base = HALO_ROWS - CONV_PAD

    def conv_chunk(c, total):
        acc = w_ref[c, 0:1, :] * hbuf[c, base:base + tm, :]
        for k in range(1, CONV_WIDTH):
            acc = acc + w_ref[c, k:k + 1, :] * hbuf[c, base + k:base + k + tm, :]
        acc = acc + b_ref[c]
        cbuf[c] = acc
        return total + acc

    ch = nch * LANES
    total = lax.fori_loop(0, nch, conv_chunk, jnp.zeros((tm, LANES), F32))
    mu = jnp.sum(total, axis=-1, keepdims=True) / ch

    def sq_dev(c, total):
        d = cbuf[c] - mu
        return total + d * d

    dev = lax.fori_loop(0, nch, sq_dev, jnp.zeros((tm, LANES), F32))
    inv = lax.rsqrt(jnp.sum(dev, axis=-1, keepdims=True) / ch + NORM_EPS)
    for c in range(nch):
        hn = (cbuf[c] - mu) * inv * g_ref[c] + beta_ref[c]
        o_ref[:, c * LANES:(c + 1) * LANES] = _silu(hn).astype(o_ref.dtype)


def _conv_front(h3, w3, b3, g3, beta3, layer, tm=256):
    nch, m, _ = h3.shape
    n_tiles = m // tm
    per = tm // HALO_ROWS
    n_halo = m // HALO_ROWS
    vec = pl.BlockSpec((nch, 1, LANES), lambda i: (layer, 0, 0))
    return pl.pallas_call(
        functools.partial(_conv_kernel, n_tiles),
        grid=(n_tiles,),
        in_specs=[pl.BlockSpec((nch, HALO_ROWS, LANES),
                               lambda i: (0, jnp.maximum(i * per - 1, 0), 0)),
                  pl.BlockSpec((nch, tm, LANES), lambda i: (0, i, 0)),
                  pl.BlockSpec((nch, HALO_ROWS, LANES),
                               lambda i: (0, jnp.minimum((i + 1) * per, n_halo - 1), 0)),
                  pl.BlockSpec((nch, CONV_WIDTH, LANES), lambda i: (layer, 0, 0)),
                  vec, vec, vec],
        out_specs=pl.BlockSpec((tm, nch * LANES), lambda i: (i, 0)),
        out_shape=jax.ShapeDtypeStruct((m, nch * LANES), BF16),
        scratch_shapes=[pltpu.VMEM((nch, tm + 2 * HALO_ROWS, LANES), F32),
                        pltpu.VMEM((nch, tm, LANES), F32)],
        compiler_params=_params(1),
        name="conv_front",
    )(h3, h3, h3, w3, b3, g3, beta3)


def _pw_kernel(x_ref, w_ref, b_ref, g_ref, o_ref):
    y = _bdot(x_ref[...], w_ref[...].astype(BF16)) + b_ref[...]
    ms = jnp.mean(y * y, axis=-1, keepdims=True)
    o_ref[...] = (y * lax.rsqrt(ms + NORM_EPS) * g_ref[...]).astype(o_ref.dtype)


def _pointwise(x, w_pw, b3, g3, layer, bm=512):
    m, k = x.shape
    n = w_pw.shape[2]
    vec = pl.BlockSpec((None, 1, n), lambda i: (layer, 0, 0))
    return pl.pallas_call(
        _pw_kernel,
        grid=(m // bm,),
        in_specs=[pl.BlockSpec((bm, k), lambda i: (i, 0)),
                  pl.BlockSpec((None, k, n), lambda i: (layer, 0, 0)),
                  vec, vec],
        out_specs=pl.BlockSpec((bm, n), lambda i: (i, 0)),
        out_shape=jax.ShapeDtypeStruct((m, n), BF16),
        compiler_params=_params(1),
        name="pointwise",
    )(x, w_pw, b3, g3)


def _chunk_major(v):
    lead, c = v.shape[0], v.shape[-1]
    mid = v.shape[1:-1]
    v = v.reshape(lead, -1, c // LANES, LANES)
    v = jnp.swapaxes(v, 1, 2)
    return v.reshape(lead * (c // LANES), *(mid or (1,)), LANES)


def kernel(x, positions, ffn1_norm, ffn1_w_gate, ffn1_w_up, ffn1_w_down, mix_norm, w_in,
           attn_sink, conv_w_dw, conv_b_dw, conv_ln_g, conv_ln_b, conv_w_pw, conv_b_pw,
           attn_out_gain, conv_out_gain, w_out, ffn2_norm, ffn2_w_gate, ffn2_w_up,
           ffn2_w_down, final_norm):
    b, s, d = x.shape
    depth = w_in.shape[0]
    d_ff = ffn1_w_gate.shape[2]
    d_attn = attn_out_gain.shape[1]
    d_conv = conv_out_gain.shape[1]
    assert b == 1 and positions.shape == (b, s)
    assert w_in.shape[2] == d_attn + 2 * D_KV + 2 * d_conv

    as3 = lambda v: v.reshape(v.shape[0], 1, v.shape[1])
    ffn1_norm3, ffn2_norm3, mix_norm3 = as3(ffn1_norm), as3(ffn2_norm), as3(mix_norm)
    b_pw3, attn_gain3, conv_gain3 = as3(conv_b_pw), as3(attn_out_gain), as3(conv_out_gain)
    w_dw_c, b_dw_c = _chunk_major(conv_w_dw), _chunk_major(conv_b_dw)
    ln_g_c, ln_b_c = _chunk_major(conv_ln_g), _chunk_major(conv_ln_b)

    bn_glu = 256
    tables = _rope_tables(positions)
    h = x.reshape(s, d)
    hg, ssq = _norm_prep(h, ffn1_norm3, 0)

    def ffn(h, hg, ssq, wg, wu, wd, l, next_gain):
        act = _gated_mm(hg, ssq, wg, wu, l, 0, 0, d_ff, _swiglu_act, BF16,
                        bm=1024, bn=512, n_slabs=1)
        return _resid_mm([act], wd, l, h, FFN_RESID, next_gain, bm=1024, bn=256)

    for l in range(depth):
        h, hg, ssq = ffn(h, hg, ssq, ffn1_w_gate, ffn1_w_up, ffn1_w_down, l, (mix_norm3, l))
        qkv = _qkv_proj(hg, ssq, w_in, l, tables, d_attn, bm=2048, bn=512, n_slabs=4,
                        lhs_buffers=1)
        u_off = (d_attn + 2 * D_KV) // bn_glu
        glu = _gated_mm(hg, ssq, w_in, w_in, l, u_off, u_off + d_conv // bn_glu, d_conv,
                        _glu_act, F32, bm=2048, bn=bn_glu, n_slabs=2, chunk_major=True)
        attn = _attention(qkv, attn_sink, attn_gain3, l, d_attn)
        conv = _conv_front(glu, w_dw_c, b_dw_c, ln_g_c, ln_b_c, l)
        conv = _pointwise(conv, conv_w_pw, b_pw3, conv_gain3, l)
        h, hg, ssq = _resid_mm([attn, conv], w_out, l, h, 1.0, (ffn2_norm3, l),
                               bm=1024, bn=512)
        if l + 1 < depth:
            h, hg, ssq = ffn(h, hg, ssq, ffn2_w_gate, ffn2_w_up, ffn2_w_down, l,
                             (ffn1_norm3, l + 1))
        else:
            h = ffn(h, hg, ssq, ffn2_w_gate, ffn2_w_up, ffn2_w_down, l, None)

    out = _rmsnorm(h, final_norm.reshape(1, 1, d), 0, F32)
    return out.reshape(b, s, d)
```

```python
import functools
import math

import jax
import jax.numpy as jnp
import numpy as np
from jax import lax
from jax.experimental import pallas as pl
from jax.experimental.pallas import tpu as pltpu

F32 = jnp.float32
BF16 = jnp.bfloat16

LANES = 128
HEAD_DIM = 128
N_KV_HEADS = 4
GQA_GROUP = 4
D_KV = N_KV_HEADS * HEAD_DIM
CONV_WIDTH = 31
CONV_PAD = CONV_WIDTH // 2
ATTN_BLOCK = 128
ROPE_DIM = HEAD_DIM // 4
ROPE_THETA = 500000.0
NORM_EPS = 1e-6
FFN_RESID = 0.5
MASK_VALUE = -1e30
LOG2E = math.log2(math.e)

VMEM_LIMIT_BYTES = 62 * 1024 * 1024
HALO_ROWS = 16


def _params(n_axes):
    return pltpu.CompilerParams(
        dimension_semantics=("arbitrary",) * n_axes,
        vmem_limit_bytes=VMEM_LIMIT_BYTES)


def _sigmoid(x):
    return 1.0 / (1.0 + jnp.exp(-x))


def _silu(x):
    return x * _sigmoid(x)


def _bdot(a, b):
    return jnp.dot(a, b, preferred_element_type=F32)


def _lane_fold(x):
    acc = x[:, :LANES]
    for c in range(1, x.shape[1] // LANES):
        acc = acc + x[:, c * LANES:(c + 1) * LANES]
    return acc


def _row_rstd(ssq_ref, d):
    return lax.rsqrt(jnp.sum(ssq_ref[...], axis=-1, keepdims=True) / d + NORM_EPS)


def _norm_prep_kernel(x_ref, g_ref, hg_ref, ssq_ref):
    x = x_ref[...]
    hg_ref[...] = (x * g_ref[...]).astype(hg_ref.dtype)
    ssq_ref[...] = _lane_fold(x * x)


def _norm_prep(x, gain3, layer, rows=512):
    m, d = x.shape
    return pl.pallas_call(
        _norm_prep_kernel,
        grid=(m // rows,),
        in_specs=[pl.BlockSpec((rows, d), lambda i: (i, 0)),
                  pl.BlockSpec((None, 1, d), lambda i: (layer, 0, 0))],
        out_specs=[pl.BlockSpec((rows, d), lambda i: (i, 0)),
                   pl.BlockSpec((rows, LANES), lambda i: (i, 0))],
        out_shape=[jax.ShapeDtypeStruct((m, d), BF16),
                   jax.ShapeDtypeStruct((m, LANES), F32)],
        compiler_params=_params(1),
        name="norm_prep",
    )(x, gain3)


def _rmsnorm_kernel(x_ref, g_ref, o_ref):
    x = x_ref[...]
    ms = jnp.mean(x * x, axis=-1, keepdims=True)
    o_ref[...] = (x * lax.rsqrt(ms + NORM_EPS) * g_ref[...]).astype(o_ref.dtype)


def _rmsnorm(x, gain3, layer, out_dtype, rows=512):
    m, d = x.shape
    return pl.pallas_call(
        _rmsnorm_kernel,
        grid=(m // rows,),
        in_specs=[pl.BlockSpec((rows, d), lambda i: (i, 0)),
                  pl.BlockSpec((None, 1, d), lambda i: (layer, 0, 0))],
        out_specs=pl.BlockSpec((rows, d), lambda i: (i, 0)),
        out_shape=jax.ShapeDtypeStruct((m, d), out_dtype),
        compiler_params=_params(1),
        name="rmsnorm",
    )(x, gain3)


def _gated_mm_kernel(act, chunk_major, n_slabs, x_ref, ssq_ref, wa_ref, wb_ref, o_ref):
    wa = wa_ref[...].astype(BF16)
    wb = wb_ref[...].astype(BF16)
    bm, k = x_ref.shape
    sm = bm // n_slabs
    for s in range(n_slabs):
        rows = slice(s * sm, (s + 1) * sm)
        x = x_ref[rows, :]
        rstd = _row_rstd(ssq_ref.at[rows, :], k)
        y = act(_bdot(x, wa) * rstd, _bdot(x, wb) * rstd).astype(o_ref.dtype)
        if chunk_major:
            for c in range(y.shape[1] // LANES):
                o_ref[c, rows, :] = y[:, c * LANES:(c + 1) * LANES]
        else:
            o_ref[rows, :] = y


def _gated_mm(hg, ssq, wa, wb, layer, col_a, col_b, n_out, act, out_dtype, bm, bn, n_slabs,
              chunk_major=False):
    m, k = hg.shape
    if chunk_major:
        out_spec = pl.BlockSpec((bn // LANES, bm, LANES), lambda i, j: (j, i, 0))
        out_shape = jax.ShapeDtypeStruct((n_out // LANES, m, LANES), out_dtype)
    else:
        out_spec = pl.BlockSpec((bm, bn), lambda i, j: (i, j))
        out_shape = jax.ShapeDtypeStruct((m, n_out), out_dtype)
    return pl.pallas_call(
        functools.partial(_gated_mm_kernel, act, chunk_major, n_slabs),
        grid=(m // bm, n_out // bn),
        in_specs=[pl.BlockSpec((bm, k), lambda i, j: (i, 0)),
                  pl.BlockSpec((bm, LANES), lambda i, j: (i, 0)),
                  pl.BlockSpec((None, k, bn), lambda i, j: (layer, 0, j + col_a)),
                  pl.BlockSpec((None, k, bn), lambda i, j: (layer, 0, j + col_b))],
        out_specs=out_spec,
        out_shape=out_shape,
        compiler_params=_params(2),
        name="gated_mm",
    )(hg, ssq, wa, wb)


def _swiglu_act(g, u):
    return _silu(g) * u


def _glu_act(a, gate):
    return a * _sigmoid(gate)


def _resid_mm_kernel(n_lhs, scale, emit_norm, *refs):
    lhs_refs = refs[:n_lhs]
    w_refs = refs[n_lhs:2 * n_lhs]
    r_ref = refs[2 * n_lhs]
    acc = _bdot(lhs_refs[0][...], w_refs[0][...].astype(BF16))
    for t in range(1, n_lhs):
        acc = acc + _bdot(lhs_refs[t][...], w_refs[t][...].astype(BF16))
    if scale != 1.0:
        acc = scale * acc
    h = r_ref[...] + acc
    if not emit_norm:
        refs[2 * n_lhs + 1][...] = h
        return
    g_ref, o_ref, hg_ref, ssq_ref = refs[2 * n_lhs + 1:]
    j = pl.program_id(1)
    bn = o_ref.shape[1]
    o_ref[...] = h
    g = g_ref[:, pl.ds(pl.multiple_of(j * bn, bn), bn)]
    hg_ref[...] = (h * g).astype(hg_ref.dtype)
    part = _lane_fold(h * h)

    @pl.when(j == 0)
    def _():
        ssq_ref[...] = part

    @pl.when(j > 0)
    def _():
        ssq_ref[...] += part


def _resid_mm(lhs_list, w, layer, resid, scale, next_gain, bm, bn):
    m, kk = lhs_list[0].shape
    n = resid.shape[1]
    n_lhs = len(lhs_list)
    tile = pl.BlockSpec((bm, bn), lambda i, j: (i, j))
    in_specs = [pl.BlockSpec((bm, kk), lambda i, j: (i, 0)) for _ in lhs_list]
    in_specs += [pl.BlockSpec((None, kk, bn), functools.partial(
        lambda t, i, j: (layer, t, j), t)) for t in range(n_lhs)]
    in_specs += [tile]
    args = [*lhs_list, *([w] * n_lhs), resid]
    h_shape = jax.ShapeDtypeStruct((m, n), F32)
    if next_gain is None:
        out_specs, out_shape = tile, h_shape
    else:
        gain3, gl = next_gain
        in_specs += [pl.BlockSpec((None, 1, n), lambda i, j: (gl, 0, 0))]
        args += [gain3]
        out_specs = [tile, tile, pl.BlockSpec((bm, LANES), lambda i, j: (i, 0))]
        out_shape = [h_shape, jax.ShapeDtypeStruct((m, n), BF16),
                     jax.ShapeDtypeStruct((m, LANES), F32)]
    return pl.pallas_call(
        functools.partial(_resid_mm_kernel, n_lhs, scale, next_gain is not None),
        grid=(m // bm, n // bn),
        in_specs=in_specs,
        out_specs=out_specs,
        out_shape=out_shape,
        compiler_params=_params(2),
        name="resid_mm",
    )(*args)


def _rope_table_kernel(pos_ref, freq_ref, c_ref, slo_ref, shi_ref):
    half = ROPE_DIM // 2
    ang = pos_ref[...].astype(F32) * freq_ref[...]
    cos, sin = jnp.cos(ang), jnp.sin(ang)
    lane = lax.broadcasted_iota(jnp.int32, ang.shape, 1)
    c_ref[...] = jnp.where(lane < ROPE_DIM, cos, 1.0)
    slo_ref[...] = jnp.where(lane < half, -sin, 0.0)
    shi_ref[...] = jnp.where((lane >= half) & (lane < ROPE_DIM), sin, 0.0)


def _rope_tables(positions, rows=512):
    m = positions.size
    half = ROPE_DIM // 2
    inv_freq = ROPE_THETA ** (-np.arange(half, dtype=np.float32) * np.float32(2.0) / ROPE_DIM)
    lane_freq = np.zeros((1, HEAD_DIM), np.float32)
    lane_freq[0, :half] = inv_freq
    lane_freq[0, half:ROPE_DIM] = inv_freq
    tab = jax.ShapeDtypeStruct((m, HEAD_DIM), F32)
    spec = pl.BlockSpec((rows, HEAD_DIM), lambda i: (i, 0))
    return pl.pallas_call(
        _rope_table_kernel,
        grid=(m // rows,),
        in_specs=[pl.BlockSpec((rows, 1), lambda i: (i, 0)),
                  pl.BlockSpec((1, HEAD_DIM), lambda i: (0, 0))],
        out_specs=[spec, spec, spec],
        out_shape=[tab, tab, tab],
        compiler_params=_params(1),
        name="rope_tables",
    )(positions.reshape(m, 1), jnp.asarray(lane_freq))


def _qkv_kernel(n_rope_tiles, n_slabs, x_ref, ssq_ref, w_ref, c_ref, slo_ref, shi_ref, o_ref):
    half = ROPE_DIM // 2
    is_rope = (pl.program_id(1) < n_rope_tiles).astype(F32)
    w = w_ref[...].astype(BF16)
    bm, bn = o_ref.shape
    sm = bm // n_slabs
    for s in range(n_slabs):
        rows = slice(s * sm, (s + 1) * sm)
        rstd = _row_rstd(ssq_ref.at[rows, :], x_ref.shape[1])
        y = _bdot(x_ref[rows, :], w) * rstd
        c = 1.0 + is_rope * (c_ref[rows, :] - 1.0)
        slo = is_rope * slo_ref[rows, :]
        shi = is_rope * shi_ref[rows, :]
        for h in range(bn // HEAD_DIM):
            yh = y[:, h * HEAD_DIM:(h + 1) * HEAD_DIM]
            rot = (yh * c + pltpu.roll(yh, HEAD_DIM - half, 1) * slo
                   + pltpu.roll(yh, half, 1) * shi)
            o_ref[rows, h * HEAD_DIM:(h + 1) * HEAD_DIM] = rot.astype(o_ref.dtype)


def _qkv_proj(hg, ssq, w_in, layer, tables, d_attn, bm, bn, n_slabs, lhs_buffers):
    m, k = hg.shape
    n_out = d_attn + 2 * D_KV
    n_rope_tiles = (d_attn + D_KV) // bn
    tspec = pl.BlockSpec((bm, HEAD_DIM), lambda i, j: (i, 0))
    return pl.pallas_call(
        functools.partial(_qkv_kernel, n_rope_tiles, n_slabs),
        grid=(m // bm, n_out // bn),
        in_specs=[pl.BlockSpec((bm, k), lambda i, j: (i, 0),
                               pipeline_mode=pl.Buffered(lhs_buffers)),
                  pl.BlockSpec((bm, LANES), lambda i, j: (i, 0)),
                  pl.BlockSpec((None, k, bn), lambda i, j: (layer, 0, j)),
                  tspec, tspec, tspec],
        out_specs=pl.BlockSpec((bm, bn), lambda i, j: (i, j)),
        out_shape=jax.ShapeDtypeStruct((m, n_out), BF16),
        compiler_params=_params(2),
        name="qkv_proj",
    )(hg, ssq, w_in, *tables)


def _attn_kernel(layer, n_blocks, sink_ref, q_ref, kp_ref, kc_ref, kn_ref,
                 vp_ref, vc_ref, vn_ref, g_ref, o_ref, acc_ref):
    n = pl.program_id(0)
    blk = ATTN_BLOCK
    r = lax.broadcasted_iota(jnp.int32, (blk, blk), 0)
    c = lax.broadcasted_iota(jnp.int32, (blk, blk), 1)
    prev_ok = (c >= r) & (n > 0)
    next_ok = (c <= r) & (n < n_blocks - 1)
    scale2 = HEAD_DIM ** -0.5 * LOG2E
    ones = jnp.ones((3 * blk, HEAD_DIM), BF16)
    ss = jnp.zeros((blk, 1), F32)
    for h in range(N_KV_HEADS):
        hs = slice(h * HEAD_DIM, (h + 1) * HEAD_DIM)
        kw = jnp.concatenate([kp_ref[:, hs], kc_ref[:, hs], kn_ref[:, hs]], axis=0)
        vw = jnp.concatenate([vp_ref[:, hs], vc_ref[:, hs], vn_ref[:, hs]], axis=0)
        vw1 = jnp.concatenate([vw, ones], axis=1)
        for g in range(GQA_GROUP):
            qs = slice((h * GQA_GROUP + g) * HEAD_DIM, (h * GQA_GROUP + g + 1) * HEAD_DIM)
            s = lax.dot_general(q_ref[:, qs], kw, (((1,), (1,)), ((), ())),
                                preferred_element_type=F32) * scale2
            s_prev = jnp.where(prev_ok, s[:, :blk], MASK_VALUE)
            s_cur = s[:, blk:2 * blk]
            s_next = jnp.where(next_ok, s[:, 2 * blk:], MASK_VALUE)
            sink2 = sink_ref[layer, h * GQA_GROUP + g] * LOG2E
            mx = jnp.max(jnp.maximum(jnp.maximum(s_prev, s_cur), s_next),
                         axis=-1, keepdims=True)
            mx = jnp.maximum(mx, sink2)
            p = jnp.concatenate([jnp.exp2(s_prev - mx), jnp.exp2(s_cur - mx),
                                 jnp.exp2(s_next - mx)], axis=1).astype(BF16)
            pv = _bdot(p, vw1)
            den = pv[:, HEAD_DIM:] + jnp.exp2(sink2 - mx)
            o = pv[:, :HEAD_DIM] / den
            ss = ss + jnp.sum(o * o, axis=-1, keepdims=True)
            acc_ref[:, qs] = o
    inv = lax.rsqrt(ss / acc_ref.shape[1] + NORM_EPS)
    o_ref[...] = (acc_ref[...] * inv * g_ref[...]).astype(o_ref.dtype)


def _attention(qkv, sink, gain3, layer, d_attn):
    m = qkv.shape[0]
    blk = ATTN_BLOCK
    nb = m // blk
    kcol = d_attn // D_KV
    vcol = kcol + 1
    prev = lambda n: jnp.maximum(n - 1, 0)
    nxt = lambda n: jnp.minimum(n + 1, nb - 1)
    kv = lambda rowf, col: pl.BlockSpec((blk, D_KV), lambda n: (rowf(n), col))
    same = lambda n: n
    return pl.pallas_call(
        functools.partial(_attn_kernel, layer, nb),
        grid=(nb,),
        in_specs=[pl.BlockSpec(memory_space=pltpu.SMEM),
                  pl.BlockSpec((blk, d_attn), lambda n: (n, 0)),
                  kv(prev, kcol), kv(same, kcol), kv(nxt, kcol),
                  kv(prev, vcol), kv(same, vcol), kv(nxt, vcol),
                  pl.BlockSpec((None, 1, d_attn), lambda n: (layer, 0, 0))],
        out_specs=pl.BlockSpec((blk, d_attn), lambda n: (n, 0)),
        out_shape=jax.ShapeDtypeStruct((m, d_attn), BF16),
        scratch_shapes=[pltpu.VMEM((blk, d_attn), F32)],
        compiler_params=_params(1),
        name="window_attn",
    )(sink, qkv, qkv, qkv, qkv, qkv, qkv, qkv, gain3)


def _conv_kernel(n_tiles, prev_ref, main_ref, next_ref, w_ref, b_ref,
                 g_ref, beta_ref, o_ref, hbuf, cbuf):
    i = pl.program_id(0)
    nch, tm, _ = main_ref.shape
    hbuf[:, 0:HALO_ROWS, :] = jnp.where(i > 0, prev_ref[...], 0.0)
    hbuf[:, HALO_ROWS:HALO_ROWS + tm, :] = main_ref[...]
    hbuf[:, HALO_ROWS + tm:2 * HALO_ROWS + tm, :] = jnp.where(i < n_tiles - 1, next_ref[...], 0.0)
    base = HALO_ROWS - CONV_PAD

    def conv_chunk(c, total):
        acc = w_ref[c, 0:1, :] * hbuf[c, base:base + tm, :]
        for k in range(1, CONV_WIDTH):
            acc = acc + w_ref[c, k:k + 1, :] * hbuf[c, base + k:base + k + tm, :]
        acc = acc + b_ref[c]
        cbuf[c] = acc
        return total + acc

    ch = nch * LANES
    total = lax.fori_loop(0, nch, conv_chunk, jnp.zeros((tm, LANES), F32))
    mu = jnp.sum(total, axis=-1, keepdims=True) / ch

    def sq_dev(c, total):
        d = cbuf[c] - mu
        return total + d * d

    dev = lax.fori_loop(0, nch, sq_dev, jnp.zeros((tm, LANES), F32))
    inv = lax.rsqrt(jnp.sum(dev, axis=-1, keepdims=True) / ch + NORM_EPS)
    for c in range(nch):
        hn = (cbuf[c] - mu) * inv * g_ref[c] + beta_ref[c]
        o_ref[:, c * LANES:(c + 1) * LANES] = _silu(hn).astype(o_ref.dtype)


def _conv_front(h3, w3, b3, g3, beta3, layer, tm=256):
    nch, m, _ = h3.shape
    n_tiles = m // tm
    per = tm // HALO_ROWS
    n_halo = m // HALO_ROWS
    vec = pl.BlockSpec((nch, 1, LANES), lambda i: (layer, 0, 0))
    return pl.pallas_call(
        functools.partial(_conv_kernel, n_tiles),
        grid=(n_tiles,),
        in_specs=[pl.BlockSpec((nch, HALO_ROWS, LANES),
                               lambda i: (0, jnp.maximum(i * per - 1, 0), 0)),
                  pl.BlockSpec((nch, tm, LANES), lambda i: (0, i, 0)),
                  pl.BlockSpec((nch, HALO_ROWS, LANES),
                               lambda i: (0, jnp.minimum((i + 1) * per, n_halo - 1), 0)),
                  pl.BlockSpec((nch, CONV_WIDTH, LANES), lambda i: (layer, 0, 0)),
                  vec, vec, vec],
        out_specs=pl.BlockSpec((tm, nch * LANES), lambda i: (i, 0)),
        out_shape=jax.ShapeDtypeStruct((m, nch * LANES), BF16),
        scratch_shapes=[pltpu.VMEM((nch, tm + 2 * HALO_ROWS, LANES), F32),
                        pltpu.VMEM((nch, tm, LANES), F32)],
        compiler_params=_params(1),
        name="conv_front",
    )(h3, h3, h3, w3, b3, g3, beta3)


def _pw_kernel(x_ref, w_ref, b_ref, g_ref, o_ref):
    y = _bdot(x_ref[...], w_ref[...].astype(BF16)) + b_ref[...]
    ms = jnp.mean(y * y, axis=-1, keepdims=True)
    o_ref[...] = (y * lax.rsqrt(ms + NORM_EPS) * g_ref[...]).astype(o_ref.dtype)


def _pointwise(x, w_pw, b3, g3, layer, bm=1024):
    m, k = x.shape
    n = w_pw.shape[2]
    vec = pl.BlockSpec((None, 1, n), lambda i: (layer, 0, 0))
    return pl.pallas_call(
        _pw_kernel,
        grid=(m // bm,),
        in_specs=[pl.BlockSpec((bm, k), lambda i: (i, 0)),
                  pl.BlockSpec((None, k, n), lambda i: (layer, 0, 0)),
                  vec, vec],
        out_specs=pl.BlockSpec((bm, n), lambda i: (i, 0)),
        out_shape=jax.ShapeDtypeStruct((m, n), BF16),
        compiler_params=_params(1),
        name="pointwise",
    )(x, w_pw, b3, g3)


def _chunk_major(v):
    lead, c = v.shape[0], v.shape[-1]
    mid = v.shape[1:-1]
    v = v.reshape(lead, -1, c // LANES, LANES)
    v = jnp.swapaxes(v, 1, 2)
    return v.reshape(lead * (c // LANES), *(mid or (1,)), LANES)


def kernel(x, positions, ffn1_norm, ffn1_w_gate, ffn1_w_up, ffn1_w_down, mix_norm, w_in,
           attn_sink, conv_w_dw, conv_b_dw, conv_ln_g, conv_ln_b, conv_w_pw, conv_b_pw,
           attn_out_gain, conv_out_gain, w_out, ffn2_norm, ffn2_w_gate, ffn2_w_up,
           ffn2_w_down, final_norm):
    b, s, d = x.shape
    depth = w_in.shape[0]
    d_ff = ffn1_w_gate.shape[2]
    d_attn = attn_out_gain.shape[1]
    d_conv = conv_out_gain.shape[1]
    assert b == 1 and positions.shape == (b, s)
    assert w_in.shape[2] == d_attn + 2 * D_KV + 2 * d_conv

    as3 = lambda v: v.reshape(v.shape[0], 1, v.shape[1])
    ffn1_norm3, ffn2_norm3, mix_norm3 = as3(ffn1_norm), as3(ffn2_norm), as3(mix_norm)
    b_pw3, attn_gain3, conv_gain3 = as3(conv_b_pw), as3(attn_out_gain), as3(conv_out_gain)
    w_dw_c, b_dw_c = _chunk_major(conv_w_dw), _chunk_major(conv_b_dw)
    ln_g_c, ln_b_c = _chunk_major(conv_ln_g), _chunk_major(conv_ln_b)

    bn_glu = 256
    tables = _rope_tables(positions)
    h = x.reshape(s, d)
    hg, ssq = _norm_prep(h, ffn1_norm3, 0)

    def ffn(h, hg, ssq, wg, wu, wd, l, next_gain):
        act = _gated_mm(hg, ssq, wg, wu, l, 0, 0, d_ff, _swiglu_act, BF16,
                        bm=1024, bn=512, n_slabs=1)
        return _resid_mm([act], wd, l, h, FFN_RESID, next_gain, bm=1024, bn=256)

    for l in range(depth):
        h, hg, ssq = ffn(h, hg, ssq, ffn1_w_gate, ffn1_w_up, ffn1_w_down, l, (mix_norm3, l))
        qkv = _qkv_proj(hg, ssq, w_in, l, tables, d_attn, bm=2048, bn=512, n_slabs=4,
                        lhs_buffers=1)
        u_off = (d_attn + 2 * D_KV) // bn_glu
        glu = _gated_mm(hg, ssq, w_in, w_in, l, u_off, u_off + d_conv // bn_glu, d_conv,
                        _glu_act, F32, bm=2048, bn=bn_glu, n_slabs=4, chunk_major=True)
        attn = _attention(qkv, attn_sink, attn_gain3, l, d_attn)
        conv = _conv_front(glu, w_dw_c, b_dw_c, ln_g_c, ln_b_c, l)
        conv = _pointwise(conv, conv_w_pw, b_pw3, conv_gain3, l)
        h, hg, ssq = _resid_mm([attn, conv], w_out, l, h, 1.0, (ffn2_norm3, l),
                               bm=1024, bn=512)
        if l + 1 < depth:
            h, hg, ssq = ffn(h, hg, ssq, ffn2_w_gate, ffn2_w_up, ffn2_w_down, l,
                             (ffn1_norm3, l + 1))
        else:
            h = ffn(h, hg, ssq, ffn2_w_gate, ffn2_w_up, ffn2_w_down, l, None)

    out = _rmsnorm(h, final_norm.reshape(1, 1, d), 0, F32)
    return out.reshape(b, s, d)
```
